```python
import jax, jax.numpy as jnp
from jax import lax
import numpy as np

D_MODEL = 4096
BATCH = 2
SEQ = 8192
DEPTH = 1

MEM_LEN = 256
CHUNK = 128
MIX_WIDTH = D_MODEL
A_WIDTH = MIX_WIDTH // 2
A_HEADS = 8
A_HEAD_DIM = A_WIDTH // A_HEADS
B_WIDTH = MIX_WIDTH - A_WIDTH
B_HEADS = 8
B_HEAD_DIM = B_WIDTH // B_HEADS
B_QK_WIDTH = B_HEADS * B_HEAD_DIM
IN_COLS = 2 * A_WIDTH + 2 * B_QK_WIDTH + 2 * B_WIDTH
ROPE_BASE = 10000.0
X_HEADS = 4
X_HEAD_DIM = D_MODEL // X_HEADS
N_EXPERTS = 32
TOP_K = 4
D_FF_EXPERT = D_MODEL // 4
SWIGLU_LIMIT = 7.0
SWIGLU_ALPHA = 1.702
MOE_BLOCK = 128
NORM_EPS = 1e-5
RET_NORM_EPS = 1e-6

kernel_name = 'hybrid_sgmlp_retention_xmem_moe'


def rms_norm(x, g, eps=NORM_EPS):
    xf = x.astype(jnp.float32)
    y = xf * lax.rsqrt(jnp.mean(xf * xf, axis=-1, keepdims=True) + eps)
    return (y * g.astype(jnp.float32)).astype(x.dtype)


def layer_norm(x, g, b, eps=NORM_EPS):
    xf = x.astype(jnp.float32)
    xc = xf - jnp.mean(xf, axis=-1, keepdims=True)
    var = jnp.mean(xc * xc, axis=-1, keepdims=True)
    return (xc * lax.rsqrt(var + eps) * g.astype(jnp.float32) + b.astype(jnp.float32)).astype(x.dtype)


def rotary(t, positions):
    half = t.shape[-1] // 2
    inv_freq = ROPE_BASE ** (-jnp.arange(half, dtype=jnp.float32) / half)
    ang = positions.astype(jnp.float32)[:, :, None, None] * inv_freq
    cos, sin = jnp.cos(ang), jnp.sin(ang)
    t1, t2 = t[..., :half], t[..., half:]
    return jnp.concatenate([t1 * cos - t2 * sin, t2 * cos + t1 * sin], axis=-1)


def spatial_gating_mixer(a_in, w_s, b_s, ln_g, ln_b):
    bsz, s, _ = a_in.shape
    hid = jax.nn.gelu(a_in, approximate=False)
    u, v = jnp.split(hid, 2, axis=-1)
    v = layer_norm(v, ln_g, ln_b)
    v = v.reshape(bsz, s // CHUNK, CHUNK, A_HEADS, A_HEAD_DIM)
    causal = jnp.tril(jnp.ones((CHUNK, CHUNK), dtype=bool))
    w_causal = jnp.where(causal[None], w_s, 0.0).astype(v.dtype)
    mixed = jnp.einsum('hij,bnjhd->bnihd', w_causal, v) + b_s.T.astype(v.dtype)[None, None, :, :, None]
    return u * mixed.reshape(bsz, s, A_WIDTH)


def retention_mixer(q, k, v, g, positions):
    f32 = jnp.float32
    bsz, s = q.shape[0], q.shape[1]
    n_chunks = s // CHUNK
    q = rotary(q.astype(f32), positions)
    k = rotary(k.astype(f32), positions) * (B_HEAD_DIM ** -0.5)
    v = v.astype(f32)
    log_gamma = jnp.log(1.0 - 2.0 ** (-5.0 - jnp.arange(B_HEADS, dtype=f32)))
    idx = jnp.arange(CHUNK, dtype=f32)
    diff = idx[:, None] - idx[None, :]
    decay_in = jnp.where(diff >= 0, jnp.exp(log_gamma[:, None, None] * jnp.maximum(diff, 0.0)), 0.0)
    q_decay = jnp.exp(log_gamma[:, None] * (idx + 1.0))[None, :, :, None]
    k_decay = jnp.exp(log_gamma[:, None] * (CHUNK - 1.0 - idx))[None, :, :, None]
    chunk_decay = jnp.exp(log_gamma * CHUNK)[None, :, None, None]

    def to_chunks(t):
        return t.reshape(bsz, n_chunks, CHUNK, B_HEADS, B_HEAD_DIM).transpose(1, 0, 3, 2, 4)

    def step(state, qkv):
        qc, kc, vc = qkv
        scores = jnp.einsum('bhid,bhjd->bhij', qc, kc) * decay_in
        inner = jnp.einsum('bhij,bhjd->bhid', scores, vc)
        cross = jnp.einsum('bhid,bhde->bhie', qc * q_decay, state)
        state = state * chunk_decay + jnp.einsum('bhjd,bhje->bhde', kc * k_decay, vc)
        return state, inner + cross

    state0 = jnp.zeros((bsz, B_HEADS, B_HEAD_DIM, B_HEAD_DIM), f32)
    _, out = lax.scan(step, state0, (to_chunks(q), to_chunks(k), to_chunks(v)))
    out = out.transpose(1, 0, 3, 2, 4).reshape(bsz, s, B_HEADS, B_HEAD_DIM)
    out = out * lax.rsqrt(jnp.mean(out * out, axis=-1, keepdims=True) + RET_NORM_EPS)
    out = out * jax.nn.silu(g.astype(f32))
    return out.reshape(bsz, s, B_WIDTH)


def memory_cross_attention(h, mem_n, w_q, w_kv, w_o):
    bsz, s, _ = h.shape
    m = mem_n.shape[1]
    q = (h @ w_q).reshape(bsz, s, X_HEADS, X_HEAD_DIM)
    k, v = jnp.split(mem_n @ w_kv, 2, axis=-1)
    k = k.reshape(bsz, m, X_HEADS, X_HEAD_DIM)
    v = v.reshape(bsz, m, X_HEADS, X_HEAD_DIM)
    logits = jnp.einsum('bshd,bmhd->bhsm', q, k).astype(jnp.float32) * (X_HEAD_DIM ** -0.5)
    p = jax.nn.softmax(logits, axis=-1).astype(v.dtype)
    o = jnp.einsum('bhsm,bmhd->bshd', p, v).reshape(bsz, s, D_MODEL)
    return o @ w_o


def clamped_swiglu(hh):
    x_glu = jnp.minimum(hh[:, 0::2], SWIGLU_LIMIT)
    x_lin = jnp.clip(hh[:, 1::2], -SWIGLU_LIMIT, SWIGLU_LIMIT)
    return x_glu * jax.nn.sigmoid(SWIGLU_ALPHA * x_glu) * (x_lin + 1.0)


def routed_experts(h, w_router, b_router, w1, b1, w2, b2):
    bsz, s, d = h.shape
    t = h.reshape(-1, d)
    n_tok = t.shape[0]
    logits = (t @ w_router + b_router).astype(jnp.float32)
    top_val, top_idx = lax.top_k(logits, TOP_K)
    top_w = jax.nn.softmax(top_val, axis=-1)
    n_assign = n_tok * TOP_K
    n_blocks = -(-n_assign // MOE_BLOCK) + N_EXPERTS
    n_rows = n_blocks * MOE_BLOCK
    flat_e = top_idx.reshape(-1)
    flat_tok = jnp.arange(n_assign, dtype=jnp.int32) // TOP_K
    flat_w = top_w.reshape(-1)
    order = jnp.argsort(flat_e)
    e_sorted = flat_e[order]
    counts = jnp.bincount(flat_e, length=N_EXPERTS)
    starts = jnp.cumsum(counts) - counts
    padded = (counts + MOE_BLOCK - 1) // MOE_BLOCK * MOE_BLOCK
    pends = jnp.cumsum(padded)
    pstarts = pends - padded
    dest = pstarts[e_sorted] + (jnp.arange(n_assign, dtype=jnp.int32) - starts[e_sorted])
    row_tok = jnp.zeros((n_rows,), jnp.int32).at[dest].set(flat_tok[order])
    row_w = jnp.zeros((n_rows,), jnp.float32).at[dest].set(flat_w[order])
    block_start = jnp.arange(n_blocks, dtype=jnp.int32) * MOE_BLOCK
    block_e = jnp.minimum(jnp.searchsorted(pends, block_start, side='right'), N_EXPERTS - 1)

    def expert_block(args):
        tok, e = args
        xb = t[tok]
        act = clamped_swiglu(xb @ w1[e] + b1[e])
        return act @ w2[e] + b2[e]

    y_rows = lax.map(expert_block, (row_tok.reshape(n_blocks, MOE_BLOCK), block_e))
    y_rows = y_rows.reshape(n_rows, d) * row_w[:, None].astype(y_rows.dtype)
    y = jax.ops.segment_sum(y_rows, row_tok, num_segments=n_tok)
    return y.reshape(bsz, s, d)


def setup_inputs(seed: int = 0) -> dict:
    key = jax.random.key(seed)
    ks = jax.random.split(key, 25)
    f32 = jnp.float32
    L, D = DEPTH, D_MODEL

    def nrm(k, shape, scale):
        return jax.random.normal(k, shape, f32) * scale

    def gain(k, shape):
        return 1.0 + 0.02 * jax.random.normal(k, shape, f32)

    x = nrm(ks[0], (BATCH, SEQ, D), 1.0)
    mem = nrm(ks[1], (BATCH, MEM_LEN, D), 1.0)
    offset = jax.random.randint(ks[2], (BATCH, 1), 0, 4096, dtype=jnp.int32)
    positions = offset + jnp.arange(SEQ, dtype=jnp.int32)[None, :]
    return {
        'x': x,
        'mem': mem,
        'positions': positions,
        'norm_mix_g': gain(ks[3], (L, D)),
        'w_in': nrm(ks[4], (L, D, IN_COLS), D ** -0.5),
        'w_s': nrm(ks[5], (L, A_HEADS, CHUNK, CHUNK), CHUNK ** -0.5),
        'b_s': gain(ks[6], (L, A_HEADS, CHUNK)),
        'ln_v_g': gain(ks[7], (L, A_WIDTH)),
        'ln_v_b': nrm(ks[8], (L, A_WIDTH), 0.02),
        'beta_a': gain(ks[9], (L, A_WIDTH)),
        'beta_b': gain(ks[10], (L, B_WIDTH)),
        'w_out': nrm(ks[11], (L, MIX_WIDTH, D), MIX_WIDTH ** -0.5),
        'norm_x_g': gain(ks[12], (L, D)),
        'norm_mem_g': gain(ks[13], (L, D)),
        'w_xq': nrm(ks[14], (L, D, D), D ** -0.5),
        'w_xkv': nrm(ks[15], (L, D, 2 * D), D ** -0.5),
        'w_xo': nrm(ks[16], (L, D, D), D ** -0.5),
        'norm_ffn_g': gain(ks[17], (L, D)),
        'w_router': nrm(ks[18], (L, D, N_EXPERTS), D ** -0.5),
        'b_router': nrm(ks[19], (L, N_EXPERTS), 0.01),
        'w1': nrm(ks[20], (L, N_EXPERTS, D, 2 * D_FF_EXPERT), D ** -0.5),
        'b1': nrm(ks[21], (L, N_EXPERTS, 2 * D_FF_EXPERT), 0.02),
        'w2': nrm(ks[22], (L, N_EXPERTS, D_FF_EXPERT, D), D_FF_EXPERT ** -0.5),
        'b2': nrm(ks[23], (L, N_EXPERTS, D), 0.02),
        'norm_final_g': gain(ks[24], (D,)),
    }


def reference(x, mem, positions, norm_mix_g, w_in, w_s, b_s, ln_v_g, ln_v_b, beta_a, beta_b, w_out,
              norm_x_g, norm_mem_g, w_xq, w_xkv, w_xo, norm_ffn_g, w_router, b_router, w1, b1, w2, b2,
              norm_final_g):
    bsz, s, _ = x.shape
    split_at = [2 * A_WIDTH, 2 * A_WIDTH + B_QK_WIDTH, 2 * A_WIDTH + 2 * B_QK_WIDTH,
                2 * A_WIDTH + 2 * B_QK_WIDTH + B_WIDTH]
    for l in range(DEPTH):
        h = rms_norm(x, norm_mix_g[l])
        proj = h @ w_in[l]
        a_in, q, k, vb, gb = jnp.split(proj, split_at, axis=-1)
        out_a = spatial_gating_mixer(a_in, w_s[l], b_s[l], ln_v_g[l], ln_v_b[l])
        out_b = retention_mixer(q.reshape(bsz, s, B_HEADS, B_HEAD_DIM), k.reshape(bsz, s, B_HEADS, B_HEAD_DIM),
                                vb.reshape(bsz, s, B_HEADS, B_HEAD_DIM), gb.reshape(bsz, s, B_HEADS, B_HEAD_DIM),
                                positions).astype(x.dtype)
        mixed = jnp.concatenate([out_a * beta_a[l], out_b * beta_b[l]], axis=-1)
        x = x + mixed @ w_out[l]
        x = x + memory_cross_attention(rms_norm(x, norm_x_g[l]), rms_norm(mem, norm_mem_g[l]),
                                       w_xq[l], w_xkv[l], w_xo[l])
        x = x + routed_experts(rms_norm(x, norm_ffn_g[l]), w_router[l], b_router[l], w1[l], b1[l], w2[l], b2[l])
    return rms_norm(x, norm_final_g)
```

```python
import functools
import math

import jax
import jax.numpy as jnp
from jax import lax
from jax.experimental import pallas as pl
from jax.experimental.pallas import tpu as pltpu

B_HEADS = 8
X_HEADS = 4
TOP_K = 4
ROPE_BASE = 10000.0
NORM_EPS = 1e-5
RET_NORM_EPS = 1e-6
SWIGLU_LIMIT = 7.0
SWIGLU_ALPHA = 1.702

V7X_VMEM_BYTES = 64 * 1024 * 1024
VMEM_LIMIT_BYTES = V7X_VMEM_BYTES - 6 * 1024 * 1024

F32 = jnp.float32
BF16 = jnp.bfloat16


def _params(*semantics):
    return pltpu.CompilerParams(
        dimension_semantics=semantics,
        vmem_limit_bytes=VMEM_LIMIT_BYTES,
        disable_bounds_checks=True,
    )


def _tile(n, want):
    t = min(n, want)
    while n % t:
        t //= 2
    return t


def _rms_kernel(x_ref, g_ref, o_ref):
    x = x_ref[...]
    ms = jnp.mean(x * x, axis=-1, keepdims=True)
    o_ref[...] = (x * lax.rsqrt(ms + NORM_EPS) * g_ref[...]).astype(o_ref.dtype)


def _rmsnorm(x, g, out_dtype, tm=512):
    m, d = x.shape
    tm = _tile(m, tm)
    return pl.pallas_call(
        _rms_kernel,
        grid=(m // tm,),
        in_specs=[pl.BlockSpec((tm, d), lambda i: (i, 0)),
                  pl.BlockSpec((1, d), lambda i: (0, 0))],
        out_specs=pl.BlockSpec((tm, d), lambda i: (i, 0)),
        out_shape=jax.ShapeDtypeStruct((m, d), out_dtype),
        compiler_params=_params("arbitrary"),
        name="rmsnorm",
    )(x, g.reshape(1, d))


def _mm_kernel(*refs, n_a, has_res):
    a_refs = refs[:n_a]
    w_ref = refs[n_a]
    r_ref = refs[n_a + 1] if has_res else None
    o_ref, wbf = refs[-2:]

    @pl.when(pl.program_id(1) == 0)
    def _():
        wbf[...] = w_ref[...].astype(BF16)

    acc = None
    k0 = 0
    for a_ref in a_refs:
        ka = a_ref.shape[1]
        part = jnp.dot(a_ref[...], wbf[k0:k0 + ka, :], preferred_element_type=F32)
        acc = part if acc is None else acc + part
        k0 += ka
    if has_res:
        acc = acc + r_ref[...]
    o_ref[...] = acc.astype(o_ref.dtype)


def _matmul(a_parts, w, out_dtype, res=None, tm=1024, tn=512, name="matmul"):
    m = a_parts[0].shape[0]
    k, n = w.shape
    assert sum(a.shape[1] for a in a_parts) == k
    tm = _tile(m, tm)
    tn = _tile(n, tn)
    in_specs = [pl.BlockSpec((tm, a.shape[1]), lambda j, i: (i, 0)) for a in a_parts]
    in_specs.append(pl.BlockSpec((k, tn), lambda j, i: (0, j)))
    args = list(a_parts) + [w]
    if res is not None:
        in_specs.append(pl.BlockSpec((tm, tn), lambda j, i: (i, j)))
        args.append(res)
    return pl.pallas_call(
        functools.partial(_mm_kernel, n_a=len(a_parts), has_res=res is not None),
        grid=(n // tn, m // tm),
        in_specs=in_specs,
        out_specs=pl.BlockSpec((tm, tn), lambda j, i: (i, j)),
        out_shape=jax.ShapeDtypeStruct((m, n), out_dtype),
        scratch_shapes=[pltpu.VMEM((k, tn), BF16)],
        compiler_params=_params("arbitrary", "arbitrary"),
        name=name,
    )(*args)


def _amix_kernel(a_ref, ws_ref, bst_ref, lng_ref, lnb_ref, beta_ref, o_ref, *, chunk, heads, hd):
    aw = heads * hd
    a = a_ref[...].astype(F32)
    hid = 0.5 * a * (1.0 + lax.erf(a * (2.0 ** -0.5)))
    u = hid[:, :aw]
    v = hid[:, aw:]
    mean = jnp.mean(v, axis=-1, keepdims=True)
    vc = v - mean
    var = jnp.mean(vc * vc, axis=-1, keepdims=True)
    vn = (vc * lax.rsqrt(var + NORM_EPS) * lng_ref[...] + lnb_ref[...]).astype(BF16)
    ub = u * beta_ref[...]
    rows = a.shape[0]
    row = lax.broadcasted_iota(jnp.int32, (chunk, chunk), 0)
    col = lax.broadcasted_iota(jnp.int32, (chunk, chunk), 1)
    causal = row >= col
    bst = bst_ref[...]
    for h in range(heads):
        w = jnp.where(causal, ws_ref[h], 0.0).astype(BF16)
        bias = bst[:, h:h + 1]
        for c in range(rows // chunk):
            rs = slice(c * chunk, (c + 1) * chunk)
            cs = slice(h * hd, (h + 1) * hd)
            mixed = jnp.dot(w, vn[rs, cs], preferred_element_type=F32) + bias
            o_ref[rs, cs] = (ub[rs, cs] * mixed).astype(o_ref.dtype)


def _mixer_a(proj, w_s, b_s, ln_g, ln_b, beta, rows=256):
    t = proj.shape[0]
    heads, chunk, _ = w_s.shape
    aw = ln_g.shape[0]
    hd = aw // heads
    rows = _tile(t, rows)
    assert rows % chunk == 0
    kern = functools.partial(_amix_kernel, chunk=chunk, heads=heads, hd=hd)
    return pl.pallas_call(
        kern,
        grid=(t // rows,),
        in_specs=[pl.BlockSpec((rows, 2 * aw), lambda i: (i, 0)),
                  pl.BlockSpec((heads, chunk, chunk), lambda i: (0, 0, 0)),
                  pl.BlockSpec((chunk, heads), lambda i: (0, 0)),
                  pl.BlockSpec((1, aw), lambda i: (0, 0)),
                  pl.BlockSpec((1, aw), lambda i: (0, 0)),
                  pl.BlockSpec((1, aw), lambda i: (0, 0))],
        out_specs=pl.BlockSpec((rows, aw), lambda i: (i, 0)),
        out_shape=jax.ShapeDtypeStruct((t, aw), BF16),
        compiler_params=_params("arbitrary"),
        name="mixer_a",
    )(proj, w_s, b_s.T, ln_g.reshape(1, aw), ln_b.reshape(1, aw), beta.reshape(1, aw))


def _ret_kernel(pos_ref, invf_ref, din_ref, qd_ref, kd_ref, q_ref, k_ref, v_ref, g_ref, beta_ref,
                o_ref, state, *, heads, hd, chunk):
    @pl.when(pl.program_id(1) == 0)
    def _():
        state[...] = jnp.zeros_like(state)

    half = hd // 2
    ang = pos_ref[...] * invf_ref[...]
    cos = jnp.cos(ang)
    sin = jnp.sin(ang)
    qd_all = qd_ref[...]
    kd_all = kd_ref[...]
    k_scale = hd ** -0.5

    def rot(t):
        t1 = t[:, :half]
        t2 = t[:, half:]
        return jnp.concatenate([t1 * cos - t2 * sin, t2 * cos + t1 * sin], axis=-1)

    for h in range(heads):
        cs = slice(h * hd, (h + 1) * hd)
        chunk_decay = math.exp(math.log(1.0 - 2.0 ** (-5.0 - h)) * chunk)
        qr = rot(q_ref[:, cs].astype(F32))
        kr = rot(k_ref[:, cs].astype(F32)) * k_scale
        vb = v_ref[:, cs]
        scores = lax.dot_general(qr.astype(BF16), kr.astype(BF16), (((1,), (1,)), ((), ())),
                                 preferred_element_type=F32) * din_ref[h]
        inner = jnp.dot(scores.astype(BF16), vb, preferred_element_type=F32)
        st = state[h]
        cross = jnp.dot((qr * qd_all[:, h:h + 1]).astype(BF16), st.astype(BF16),
                        preferred_element_type=F32)
        kv = lax.dot_general((kr * kd_all[:, h:h + 1]).astype(BF16), vb, (((0,), (0,)), ((), ())),
                             preferred_element_type=F32)
        state[h] = st * chunk_decay + kv
        out = inner + cross
        out = out * lax.rsqrt(jnp.mean(out * out, axis=-1, keepdims=True) + RET_NORM_EPS)
        g = g_ref[:, cs].astype(F32)
        out = out * (g * jax.nn.sigmoid(g)) * beta_ref[:, cs]
        o_ref[:, cs] = out.astype(o_ref.dtype)


def _mixer_b(proj, positions, beta, chunk, heads):
    t = proj.shape[0]
    bsz, s = positions.shape
    bw = beta.shape[0]
    hd = bw // heads
    half = hd // 2
    n_chunks = s // chunk
    assert proj.shape[1] == 6 * bw
    log_gamma = jnp.log(1.0 - 2.0 ** (-5.0 - jnp.arange(heads, dtype=F32)))
    idx = jnp.arange(chunk, dtype=F32)
    diff = idx[:, None] - idx[None, :]
    decay_in = jnp.where(diff >= 0, jnp.exp(log_gamma[:, None, None] * jnp.maximum(diff, 0.0)), 0.0)
    q_decay = jnp.exp(log_gamma[None, :] * (idx[:, None] + 1.0))
    k_decay = jnp.exp(log_gamma[None, :] * (chunk - 1.0 - idx[:, None]))
    inv_freq = (ROPE_BASE ** (-jnp.arange(half, dtype=F32) / half)).reshape(1, half)
    pos = positions.astype(F32).reshape(t, 1)

    kern = functools.partial(_ret_kernel, heads=heads, hd=hd, chunk=chunk)
    row_map = lambda b, c: (b * n_chunks + c, 0)
    const2 = lambda b, c: (0, 0)

    def col_block(jb):
        return pl.BlockSpec((chunk, bw), lambda b, c: (b * n_chunks + c, jb))

    return pl.pallas_call(
        kern,
        grid=(bsz, n_chunks),
        in_specs=[pl.BlockSpec((chunk, 1), row_map),
                  pl.BlockSpec((1, half), const2),
                  pl.BlockSpec((heads, chunk, chunk), lambda b, c: (0, 0, 0)),
                  pl.BlockSpec((chunk, heads), const2),
                  pl.BlockSpec((chunk, heads), const2),
                  col_block(2), col_block(3), col_block(4), col_block(5),
                  pl.BlockSpec((1, bw), const2)],
        out_specs=pl.BlockSpec((chunk, bw), row_map),
        out_shape=jax.ShapeDtypeStruct((t, bw), BF16),
        scratch_shapes=[pltpu.VMEM((heads, hd, hd), F32)],
        compiler_params=_params("arbitrary", "arbitrary"),
        name="mixer_b",
    )(pos, inv_freq, decay_in, q_decay, k_decay, proj, proj, proj, proj, beta.reshape(1, bw))


def _xattn_kernel(q_ref, k_ref, v_ref, o_ref, *, scale):
    s = lax.dot_general(q_ref[...], k_ref[...], (((1,), (1,)), ((), ())),
                        preferred_element_type=F32) * scale
    m = jnp.max(s, axis=-1, keepdims=True)
    p = jnp.exp(s - m)
    l = jnp.sum(p, axis=-1, keepdims=True)
    o = jnp.dot(p.astype(BF16), v_ref[...], preferred_element_type=F32)
    o_ref[...] = (o / l).astype(o_ref.dtype)


def _xattn(q, kv, bsz, heads, tq=512):
    t, d = q.shape
    s = t // bsz
    m = kv.shape[0] // bsz
    dh = d // heads
    tq = _tile(s, tq)
    nq = s // tq
    return pl.pallas_call(
        functools.partial(_xattn_kernel, scale=dh ** -0.5),
        grid=(bsz, nq, heads),
        in_specs=[pl.BlockSpec((tq, dh), lambda b, i, h: (b * nq + i, h)),
                  pl.BlockSpec((m, dh), lambda b, i, h: (b, h)),
                  pl.BlockSpec((m, dh), lambda b, i, h: (b, heads + h))],
        out_specs=pl.BlockSpec((tq, dh), lambda b, i, h: (b * nq + i, h)),
        out_shape=jax.ShapeDtypeStruct((t, d), BF16),
        compiler_params=_params("arbitrary", "arbitrary", "arbitrary"),
        name="xattn",
    )(q, kv, kv)


def _router_kernel(x_ref, g_ref, wr_ref, br_ref, h_ref, idx_ref, w_ref, rank_ref, cnt_ref, cnt, *,
                   n_exp, top_k):
    @pl.when(pl.program_id(0) == 0)
    def _():
        cnt[...] = jnp.zeros_like(cnt)

    x = x_ref[...]
    ms = jnp.mean(x * x, axis=-1, keepdims=True)
    h = x * lax.rsqrt(ms + NORM_EPS) * g_ref[...]
    h_ref[...] = h
    logits = jnp.dot(h, wr_ref[...], preferred_element_type=F32,
                     precision=lax.Precision.HIGHEST) + br_ref[...]
    tt = x.shape[0]
    lane = lax.broadcasted_iota(jnp.int32, (tt, n_exp), 1)
    work = logits
    vals, idxs = [], []
    onehot = jnp.zeros((tt, n_exp), F32)
    for _ in range(top_k):
        mx = jnp.max(work, axis=-1, keepdims=True)
        ix = jnp.min(jnp.where(work == mx, lane, n_exp), axis=-1, keepdims=True)
        sel = lane == ix
        onehot = jnp.where(sel, 1.0, onehot)
        work = jnp.where(sel, -jnp.inf, work)
        vals.append(mx)
        idxs.append(ix)
    exps = [jnp.exp(v - vals[0]) for v in vals]
    denom = exps[0]
    for e in exps[1:]:
        denom = denom + e
    r = lax.broadcasted_iota(jnp.int32, (tt, tt), 0)
    c = lax.broadcasted_iota(jnp.int32, (tt, tt), 1)
    lower = jnp.where(r > c, 1.0, 0.0).astype(BF16)
    before = jnp.dot(lower, onehot.astype(BF16), preferred_element_type=F32) + cnt[...]
    for k in range(top_k):
        idx_ref[:, k:k + 1] = idxs[k]
        w_ref[:, k:k + 1] = exps[k] / denom
        rk = jnp.sum(jnp.where(lane == idxs[k], before, 0.0), axis=-1, keepdims=True)
        rank_ref[:, k:k + 1] = rk.astype(jnp.int32)
    cnt[...] = cnt[...] + jnp.sum(onehot, axis=0, keepdims=True)
    cnt_ref[...] = cnt[...]


def _router(x, g, w_router, b_router, tt=256):
    t, d = x.shape
    n_exp = w_router.shape[1]
    tt = _tile(t, tt)
    kern = functools.partial(_router_kernel, n_exp=n_exp, top_k=TOP_K)
    tok_map = lambda i: (i, 0)
    const = lambda i: (0, 0)
    return pl.pallas_call(
        kern,
        grid=(t // tt,),
        in_specs=[pl.BlockSpec((tt, d), tok_map),
                  pl.BlockSpec((1, d), const),
                  pl.BlockSpec((d, n_exp), const),
                  pl.BlockSpec((1, n_exp), const)],
        out_specs=[pl.BlockSpec((tt, d), tok_map),
                   pl.BlockSpec((tt, TOP_K), tok_map),
                   pl.BlockSpec((tt, TOP_K), tok_map),
                   pl.BlockSpec((tt, TOP_K), tok_map),
                   pl.BlockSpec((1, n_exp), const)],
        out_shape=[jax.ShapeDtypeStruct((t, d), F32),
                   jax.ShapeDtypeStruct((t, TOP_K), jnp.int32),
                   jax.ShapeDtypeStruct((t, TOP_K), F32),
                   jax.ShapeDtypeStruct((t, TOP_K), jnp.int32),
                   jax.ShapeDtypeStruct((1, n_exp), F32)],
        scratch_shapes=[pltpu.VMEM((1, n_exp), F32)],
        compiler_params=_params("arbitrary"),
        name="router",
    )(x, g.reshape(1, d), w_router, b_router.reshape(1, n_exp))


def _expert_kernel(be_ref, nb_ref, tok_ref, h_hbm, w1g_ref, w1l_ref, b1g_ref, b1l_ref, w2_ref, b2_ref,
                   o_ref, xbuf, sem, *, tm):
    i = pl.program_id(0)
    nb = nb_ref[0]

    def issue(blk, slot):
        base = blk * tm

        def body(r, carry):
            tok = tok_ref[base + r]
            pltpu.make_async_copy(h_hbm.at[pl.ds(tok, 1), :], xbuf.at[slot, pl.ds(r, 1), :],
                                  sem.at[slot]).start()
            return carry

        lax.fori_loop(0, tm, body, 0)

    @pl.when(i == 0)
    def _():
        issue(0, 0)

    @pl.when(i + 1 < nb)
    def _():
        issue(i + 1, (i + 1) % 2)

    @pl.when(i < nb)
    def _():
        slot = i % 2
        pltpu.make_async_copy(h_hbm.at[pl.ds(0, tm), :], xbuf.at[slot], sem.at[slot]).wait()
        xb = xbuf[slot].astype(BF16)
        hg = jnp.dot(xb, w1g_ref[...], preferred_element_type=F32) + b1g_ref[...]
        hl = jnp.dot(xb, w1l_ref[...], preferred_element_type=F32) + b1l_ref[...]
        glu = jnp.minimum(hg, SWIGLU_LIMIT)
        lin = jnp.clip(hl, -SWIGLU_LIMIT, SWIGLU_LIMIT)
        act = glu * jax.nn.sigmoid(SWIGLU_ALPHA * glu) * (lin + 1.0)
        y = jnp.dot(act.astype(BF16), w2_ref[...], preferred_element_type=F32) + b2_ref[...]
        o_ref[...] = y

    @pl.when(i >= nb)
    def _():
        o_ref[...] = jnp.zeros_like(o_ref)


def _experts(h, block_e, n_used, row_tok, w1g, w1l, b1g, b1l, w2, b2, tm):
    t, d = h.shape
    n_exp, _, f = w1g.shape
    n_rows = row_tok.shape[0]
    n_blocks = n_rows // tm
    once = pl.Buffered(1)

    def wmap(i, be, nb, tok):
        return (be[i], 0, 0)

    def omap(i, be, nb, tok):
        return (i, 0)

    grid_spec = pltpu.PrefetchScalarGridSpec(
        num_scalar_prefetch=3,
        grid=(n_blocks,),
        in_specs=[pl.BlockSpec(memory_space=pl.ANY),
                  pl.BlockSpec((None, d, f), wmap, pipeline_mode=once),
                  pl.BlockSpec((None, d, f), wmap, pipeline_mode=once),
                  pl.BlockSpec((None, 1, f), wmap),
                  pl.BlockSpec((None, 1, f), wmap),
                  pl.BlockSpec((None, f, d), wmap, pipeline_mode=once),
                  pl.BlockSpec((None, 1, d), wmap)],
        out_specs=pl.BlockSpec((tm, d), omap),
        scratch_shapes=[pltpu.VMEM((2, tm, d), F32),
                        pltpu.SemaphoreType.DMA((2,))],
    )
    return pl.pallas_call(
        functools.partial(_expert_kernel, tm=tm),
        grid_spec=grid_spec,
        out_shape=jax.ShapeDtypeStruct((n_rows, d), F32),
        compiler_params=_params("arbitrary"),
        name="experts",
    )(block_e, n_used, row_tok, h, w1g, w1l, b1g, b1l, w2, b2)


def _combine_kernel(pos_ref, x_ref, w_ref, g_ref, y_hbm, o_ref, ybuf, sem, *, tc, top_k, normalize):
    i = pl.program_id(0)
    n = pl.num_programs(0)

    def issue(blk, slot):
        base = blk * (tc * top_k)

        def body(r, carry):
            for k in range(top_k):
                p = pos_ref[base + r * top_k + k]
                pltpu.make_async_copy(y_hbm.at[pl.ds(p, 1), :], ybuf.at[slot, k, pl.ds(r, 1), :],
                                      sem.at[slot]).start()
            return carry

        lax.fori_loop(0, tc, body, 0)

    @pl.when(i == 0)
    def _():
        issue(0, 0)

    @pl.when(i + 1 < n)
    def _():
        issue(i + 1, (i + 1) % 2)

    slot = i % 2
    for k in range(top_k):
        pltpu.make_async_copy(y_hbm.at[pl.ds(0, tc), :], ybuf.at[slot, k], sem.at[slot]).wait()
    w = w_ref[...]
    acc = x_ref[...]
    for k in range(top_k):
        acc = acc + ybuf[slot, k] * w[:, k:k + 1]
    if normalize:
        ms = jnp.mean(acc * acc, axis=-1, keepdims=True)
        acc = acc * lax.rsqrt(ms + NORM_EPS) * g_ref[...]
    o_ref[...] = acc


def _combine(x, top_w, pos, y_rows, g, normalize, tc=128):
    t, d = x.shape
    tc = _tile(t, tc)
    grid_spec = pltpu.PrefetchScalarGridSpec(
        num_scalar_prefetch=1,
        grid=(t // tc,),
        in_specs=[pl.BlockSpec((tc, d), lambda i, p: (i, 0)),
                  pl.BlockSpec((tc, TOP_K), lambda i, p: (i, 0)),
                  pl.BlockSpec((1, d), lambda i, p: (0, 0)),
                  pl.BlockSpec(memory_space=pl.ANY)],
        out_specs=pl.BlockSpec((tc, d), lambda i, p: (i, 0)),
        scratch_shapes=[pltpu.VMEM((2, TOP_K, tc, d), F32),
                        pltpu.SemaphoreType.DMA((2,))],
    )
    return pl.pallas_call(
        functools.partial(_combine_kernel, tc=tc, top_k=TOP_K, normalize=normalize),
        grid_spec=grid_spec,
        out_shape=jax.ShapeDtypeStruct((t, d), F32),
        compiler_params=_params("arbitrary"),
        name="combine",
    )(pos.reshape(-1), x, top_w, g.reshape(1, d), y_rows)


def _moe_block_rows(n_assign):
    return 256 if n_assign >= 256 * 32 else 128


def kernel(x, mem, positions, norm_mix_g, w_in, w_s, b_s, ln_v_g, ln_v_b, beta_a, beta_b, w_out,
           norm_x_g, norm_mem_g, w_xq, w_xkv, w_xo, norm_ffn_g, w_router, b_router, w1, b1, w2, b2,
           norm_final_g):
    bsz, s, d = x.shape
    t = bsz * s
    depth = w_in.shape[0]
    chunk = w_s.shape[2]
    n_exp = w_router.shape[2]
    xs = x.reshape(t, d)
    mem2 = mem.reshape(-1, d)

    for l in range(depth):
        h = _rmsnorm(xs, norm_mix_g[l], BF16)
        proj = _matmul([h], w_in[l], BF16, name="w_in")
        out_a = _mixer_a(proj, w_s[l], b_s[l], ln_v_g[l], ln_v_b[l], beta_a[l])
        out_b = _mixer_b(proj, positions, beta_b[l], chunk, B_HEADS)
        xs = _matmul([out_a, out_b], w_out[l], F32, res=xs, name="w_out")

        h = _rmsnorm(xs, norm_x_g[l], BF16)
        mem_n = _rmsnorm(mem2, norm_mem_g[l], BF16)
        q = _matmul([h], w_xq[l], BF16, name="w_xq")
        kv = _matmul([mem_n], w_xkv[l], BF16, name="w_xkv")
        o = _xattn(q, kv, bsz, X_HEADS)
        xs = _matmul([o], w_xo[l], F32, res=xs, name="w_xo")

        hf, top_idx, top_w, rank, counts = _router(xs, norm_ffn_g[l], w_router[l], b_router[l])
        n_assign = t * TOP_K
        tm = _moe_block_rows(n_assign)
        counts = counts.reshape(n_exp).astype(jnp.int32)
        padded = (counts + tm - 1) // tm * tm
        pends = jnp.cumsum(padded)
        pstarts = pends - padded
        pos = pstarts[top_idx] + rank
        n_blocks = -(-n_assign // tm) + n_exp
        n_rows = n_blocks * tm
        row_tok = jnp.zeros((n_rows,), jnp.int32).at[pos.reshape(-1)].set(
            jnp.arange(n_assign, dtype=jnp.int32) // TOP_K)
        block_start = jnp.arange(n_blocks, dtype=jnp.int32) * tm
        block_e = jnp.minimum(jnp.searchsorted(pends, block_start, side='right'),
                              n_exp - 1).astype(jnp.int32)
        n_used = (pends[-1:] // tm).astype(jnp.int32)
        w1l_ = w1[l]
        f = w1l_.shape[2] // 2
        w1g = w1l_[:, :, 0::2].astype(BF16)
        w1lin = w1l_[:, :, 1::2].astype(BF16)
        b1g = b1[l][:, 0::2].reshape(n_exp, 1, f)
        b1lin = b1[l][:, 1::2].reshape(n_exp, 1, f)
        y_rows = _experts(hf, block_e, n_used, row_tok, w1g, w1lin, b1g, b1lin,
                          w2[l].astype(BF16), b2[l].reshape(n_exp, 1, d), tm)
        xs = _combine(xs, top_w, pos, y_rows, norm_final_g, normalize=(l + 1 == depth))
    return xs.reshape(bsz, s, d)
```

```python
import functools
import math

import jax
import jax.numpy as jnp
from jax import lax
from jax.experimental import pallas as pl
from jax.experimental.pallas import tpu as pltpu

B_HEADS = 8
X_HEADS = 4
TOP_K = 4
ROPE_BASE = 10000.0
NORM_EPS = 1e-5
RET_NORM_EPS = 1e-6
SWIGLU_LIMIT = 7.0
SWIGLU_ALPHA = 1.702

LANES = 128

V7X_VMEM_BYTES = 64 * 1024 * 1024
VMEM_LIMIT_BYTES = V7X_VMEM_BYTES - 6 * 1024 * 1024

F32 = jnp.float32
BF16 = jnp.bfloat16


def _params(*semantics):
    return pltpu.CompilerParams(
        dimension_semantics=semantics,
        vmem_limit_bytes=VMEM_LIMIT_BYTES,
        disable_bounds_checks=True,
    )


def _tile(n, want):
    t = min(n, want)
    while n % t:
        t //= 2
    return t


def _rms_kernel(x_ref, g_ref, o_ref):
    x = x_ref[...]
    ms = jnp.mean(x * x, axis=-1, keepdims=True)
    o_ref[...] = (x * lax.rsqrt(ms + NORM_EPS) * g_ref[...]).astype(o_ref.dtype)


def _rmsnorm(x, g, out_dtype, tm=512):
    m, d = x.shape
    tm = _tile(m, tm)
    return pl.pallas_call(
        _rms_kernel,
        grid=(m // tm,),
        in_specs=[pl.BlockSpec((tm, d), lambda i: (i, 0)),
                  pl.BlockSpec((1, d), lambda i: (0, 0))],
        out_specs=pl.BlockSpec((tm, d), lambda i: (i, 0)),
        out_shape=jax.ShapeDtypeStruct((m, d), out_dtype),
        compiler_params=_params("arbitrary"),
        name="rmsnorm",
    )(x, g.reshape(1, d))


def _mm_kernel(*refs, n_a, has_res):
    a_refs = refs[:n_a]
    w_ref = refs[n_a]
    r_ref = refs[n_a + 1] if has_res else None
    o_ref, wbf = refs[-2:]

    @pl.when(pl.program_id(1) == 0)
    def _():
        wbf[...] = w_ref[...].astype(BF16)

    acc = None
    k0 = 0
    for a_ref in a_refs:
        ka = a_ref.shape[1]
        part = jnp.dot(a_ref[...], wbf[k0:k0 + ka, :], preferred_element_type=F32)
        acc = part if acc is None else acc + part
        k0 += ka
    if has_res:
        acc = acc + r_ref[...]
    o_ref[...] = acc.astype(o_ref.dtype)


def _matmul(a_parts, w, out_dtype, res=None, tm=1024, tn=512, name="matmul"):
    m = a_parts[0].shape[0]
    k, n = w.shape
    assert sum(a.shape[1] for a in a_parts) == k
    tm = _tile(m, tm)
    tn = _tile(n, tn)
    in_specs = [pl.BlockSpec((tm, a.shape[1]), lambda j, i: (i, 0)) for a in a_parts]
    in_specs.append(pl.BlockSpec((k, tn), lambda j, i: (0, j)))
    args = list(a_parts) + [w]
    if res is not None:
        in_specs.append(pl.BlockSpec((tm, tn), lambda j, i: (i, j)))
        args.append(res)
    return pl.pallas_call(
        functools.partial(_mm_kernel, n_a=len(a_parts), has_res=res is not None),
        grid=(n // tn, m // tm),
        in_specs=in_specs,
        out_specs=pl.BlockSpec((tm, tn), lambda j, i: (i, j)),
        out_shape=jax.ShapeDtypeStruct((m, n), out_dtype),
        scratch_shapes=[pltpu.VMEM((k, tn), BF16)],
        compiler_params=_params("arbitrary", "arbitrary"),
        name=name,
    )(*args)


def _amix_kernel(a_ref, ws_ref, bst_ref, lng_ref, lnb_ref, beta_ref, o_ref, *, chunk, heads, hd):
    aw = heads * hd
    a = a_ref[...].astype(F32)
    hid = 0.5 * a * (1.0 + lax.erf(a * (2.0 ** -0.5)))
    u = hid[:, :aw]
    v = hid[:, aw:]
    mean = jnp.mean(v, axis=-1, keepdims=True)
    vc = v - mean
    var = jnp.mean(vc * vc, axis=-1, keepdims=True)
    vn = (vc * lax.rsqrt(var + NORM_EPS) * lng_ref[...] + lnb_ref[...]).astype(BF16)
    ub = u * beta_ref[...]
    rows = a.shape[0]
    row = lax.broadcasted_iota(jnp.int32, (chunk, chunk), 0)
    col = lax.broadcasted_iota(jnp.int32, (chunk, chunk), 1)
    causal = row >= col
    bst = bst_ref[...]
    for h in range(heads):
        w = jnp.where(causal, ws_ref[h], 0.0).astype(BF16)
        bias = bst[:, h:h + 1]
        for c in range(rows // chunk):
            rs = slice(c * chunk, (c + 1) * chunk)
            cs = slice(h * hd, (h + 1) * hd)
            mixed = jnp.dot(w, vn[rs, cs], preferred_element_type=F32) + bias
            o_ref[rs, cs] = (ub[rs, cs] * mixed).astype(o_ref.dtype)


def _mixer_a(proj, w_s, b_s, ln_g, ln_b, beta, rows=256):
    t = proj.shape[0]
    heads, chunk, _ = w_s.shape
    aw = ln_g.shape[0]
    hd = aw // heads
    rows = _tile(t, rows)
    assert rows % chunk == 0
    kern = functools.partial(_amix_kernel, chunk=chunk, heads=heads, hd=hd)
    return pl.pallas_call(
        kern,
        grid=(t // rows,),
        in_specs=[pl.BlockSpec((rows, 2 * aw), lambda i: (i, 0)),
                  pl.BlockSpec((heads, chunk, chunk), lambda i: (0, 0, 0)),
                  pl.BlockSpec((chunk, heads), lambda i: (0, 0)),
                  pl.BlockSpec((1, aw), lambda i: (0, 0)),
                  pl.BlockSpec((1, aw), lambda i: (0, 0)),
                  pl.BlockSpec((1, aw), lambda i: (0, 0))],
        out_specs=pl.BlockSpec((rows, aw), lambda i: (i, 0)),
        out_shape=jax.ShapeDtypeStruct((t, aw), BF16),
        compiler_params=_params("arbitrary"),
        name="mixer_a",
    )(proj, w_s, b_s.T, ln_g.reshape(1, aw), ln_b.reshape(1, aw), beta.reshape(1, aw))


def _ret_kernel(pos_ref, invf_ref, din_ref, qd_ref, kd_ref, q_ref, k_ref, v_ref, g_ref, beta_ref,
                o_ref, state, *, heads, hd, chunk):
    @pl.when(pl.program_id(1) == 0)
    def _():
        state[...] = jnp.zeros_like(state)

    half = hd // 2
    ang = pos_ref[...] * invf_ref[...]
    cos = jnp.cos(ang)
    sin = jnp.sin(ang)
    qd_all = qd_ref[...]
    kd_all = kd_ref[...]
    k_scale = hd ** -0.5

    def rot(t):
        t1 = t[:, :half]
        t2 = t[:, half:]
        return jnp.concatenate([t1 * cos - t2 * sin, t2 * cos + t1 * sin], axis=-1)

    for h in range(heads):
        cs = slice(h * hd, (h + 1) * hd)
        chunk_decay = math.exp(math.log(1.0 - 2.0 ** (-5.0 - h)) * chunk)
        qr = rot(q_ref[:, cs].astype(F32))
        kr = rot(k_ref[:, cs].astype(F32)) * k_scale
        vb = v_ref[:, cs]
        scores = lax.dot_general(qr.astype(BF16), kr.astype(BF16), (((1,), (1,)), ((), ())),
                                 preferred_element_type=F32) * din_ref[h]
        inner = jnp.dot(scores.astype(BF16), vb, preferred_element_type=F32)
        st = state[h]
        cross = jnp.dot((qr * qd_all[:, h:h + 1]).astype(BF16), st.astype(BF16),
                        preferred_element_type=F32)
        kv = lax.dot_general((kr * kd_all[:, h:h + 1]).astype(BF16), vb, (((0,), (0,)), ((), ())),
                             preferred_element_type=F32)
        state[h] = st * chunk_decay + kv
        out = inner + cross
        out = out * lax.rsqrt(jnp.mean(out * out, axis=-1, keepdims=True) + RET_NORM_EPS)
        g = g_ref[:, cs].astype(F32)
        out = out * (g * jax.nn.sigmoid(g)) * beta_ref[:, cs]
        o_ref[:, cs] = out.astype(o_ref.dtype)


def _mixer_b(proj, positions, beta, chunk, heads):
    t = proj.shape[0]
    bsz, s = positions.shape
    bw = beta.shape[0]
    hd = bw // heads
    half = hd // 2
    n_chunks = s // chunk
    assert proj.shape[1] == 6 * bw
    log_gamma = jnp.log(1.0 - 2.0 ** (-5.0 - jnp.arange(heads, dtype=F32)))
    idx = jnp.arange(chunk, dtype=F32)
    diff = idx[:, None] - idx[None, :]
    decay_in = jnp.where(diff >= 0, jnp.exp(log_gamma[:, None, None] * jnp.maximum(diff, 0.0)), 0.0)
    q_decay = jnp.exp(log_gamma[None, :] * (idx[:, None] + 1.0))
    k_decay = jnp.exp(log_gamma[None, :] * (chunk - 1.0 - idx[:, None]))
    inv_freq = (ROPE_BASE ** (-jnp.arange(half, dtype=F32) / half)).reshape(1, half)
    pos = positions.astype(F32).reshape(t, 1)

    kern = functools.partial(_ret_kernel, heads=heads, hd=hd, chunk=chunk)
    row_map = lambda b, c: (b * n_chunks + c, 0)
    const2 = lambda b, c: (0, 0)

    def col_block(jb):
        return pl.BlockSpec((chunk, bw), lambda b, c: (b * n_chunks + c, jb))

    return pl.pallas_call(
        kern,
        grid=(bsz, n_chunks),
        in_specs=[pl.BlockSpec((chunk, 1), row_map),
                  pl.BlockSpec((1, half), const2),
                  pl.BlockSpec((heads, chunk, chunk), lambda b, c: (0, 0, 0)),
                  pl.BlockSpec((chunk, heads), const2),
                  pl.BlockSpec((chunk, heads), const2),
                  col_block(2), col_block(3), col_block(4), col_block(5),
                  pl.BlockSpec((1, bw), const2)],
        out_specs=pl.BlockSpec((chunk, bw), row_map),
        out_shape=jax.ShapeDtypeStruct((t, bw), BF16),
        scratch_shapes=[pltpu.VMEM((heads, hd, hd), F32)],
        compiler_params=_params("arbitrary", "arbitrary"),
        name="mixer_b",
    )(pos, inv_freq, decay_in, q_decay, k_decay, proj, proj, proj, proj, beta.reshape(1, bw))


def _xattn_kernel(q_ref, k_ref, v_ref, o_ref, *, scale):
    s = lax.dot_general(q_ref[...], k_ref[...], (((1,), (1,)), ((), ())),
                        preferred_element_type=F32) * scale
    m = jnp.max(s, axis=-1, keepdims=True)
    p = jnp.exp(s - m)
    l = jnp.sum(p, axis=-1, keepdims=True)
    o = jnp.dot(p.astype(BF16), v_ref[...], preferred_element_type=F32)
    o_ref[...] = (o / l).astype(o_ref.dtype)


def _xattn(q, kv, bsz, heads, tq=512):
    t, d = q.shape
    s = t // bsz
    m = kv.shape[0] // bsz
    dh = d // heads
    tq = _tile(s, tq)
    nq = s // tq
    return pl.pallas_call(
        functools.partial(_xattn_kernel, scale=dh ** -0.5),
        grid=(bsz, nq, heads),
        in_specs=[pl.BlockSpec((tq, dh), lambda b, i, h: (b * nq + i, h)),
                  pl.BlockSpec((m, dh), lambda b, i, h: (b, h)),
                  pl.BlockSpec((m, dh), lambda b, i, h: (b, heads + h))],
        out_specs=pl.BlockSpec((tq, dh), lambda b, i, h: (b * nq + i, h)),
        out_shape=jax.ShapeDtypeStruct((t, d), BF16),
        compiler_params=_params("arbitrary", "arbitrary", "arbitrary"),
        name="xattn",
    )(q, kv, kv)


def _router_kernel(x_ref, g_ref, wr_ref, br_ref, h_ref, idx_ref, w_ref, rank_ref, cnt_ref, cnt, *,
                   n_exp, top_k):
    @pl.when(pl.program_id(0) == 0)
    def _():
        cnt[...] = jnp.zeros_like(cnt)

    x = x_ref[...]
    ms = jnp.mean(x * x, axis=-1, keepdims=True)
    h = x * lax.rsqrt(ms + NORM_EPS) * g_ref[...]
    h_ref[...] = h
    logits = jnp.dot(h, wr_ref[...], preferred_element_type=F32,
                     precision=lax.Precision.HIGHEST) + br_ref[...]
    tt = x.shape[0]
    lane = lax.broadcasted_iota(jnp.int32, (tt, n_exp), 1)
    work = logits
    vals, idxs = [], []
    onehot = jnp.zeros((tt, n_exp), F32)
    for _ in range(top_k):
        mx = jnp.max(work, axis=-1, keepdims=True)
        ix = jnp.min(jnp.where(work == mx, lane, n_exp), axis=-1, keepdims=True)
        sel = lane == ix
        onehot = jnp.where(sel, 1.0, onehot)
        work = jnp.where(sel, -jnp.inf, work)
        vals.append(mx)
        idxs.append(ix)
    exps = [jnp.exp(v - vals[0]) for v in vals]
    denom = exps[0]
    for e in exps[1:]:
        denom = denom + e
    r = lax.broadcasted_iota(jnp.int32, (tt, tt), 0)
    c = lax.broadcasted_iota(jnp.int32, (tt, tt), 1)
    lower = jnp.where(r > c, 1.0, 0.0).astype(BF16)
    before = jnp.dot(lower, onehot.astype(BF16), preferred_element_type=F32) + cnt[...]
    for k in range(top_k):
        idx_ref[:, k:k + 1] = idxs[k]
        w_ref[:, k:k + 1] = exps[k] / denom
        rk = jnp.sum(jnp.where(lane == idxs[k], before, 0.0), axis=-1, keepdims=True)
        rank_ref[:, k:k + 1] = rk.astype(jnp.int32)
    cnt[...] = cnt[...] + jnp.sum(onehot, axis=0, keepdims=True)
    cnt_ref[...] = cnt[...]


def _router(x, g, w_router, b_router, tt=256):
    t, d = x.shape
    n_exp = w_router.shape[1]
    tt = _tile(t, tt)
    kern = functools.partial(_router_kernel, n_exp=n_exp, top_k=TOP_K)
    tok_map = lambda i: (i, 0)
    const = lambda i: (0, 0)
    return pl.pallas_call(
        kern,
        grid=(t // tt,),
        in_specs=[pl.BlockSpec((tt, d), tok_map),
                  pl.BlockSpec((1, d), const),
                  pl.BlockSpec((d, n_exp), const),
                  pl.BlockSpec((1, n_exp), const)],
        out_specs=[pl.BlockSpec((tt, d), tok_map),
                   pl.BlockSpec((tt, TOP_K), tok_map),
                   pl.BlockSpec((tt, TOP_K), tok_map),
                   pl.BlockSpec((tt, TOP_K), tok_map),
                   pl.BlockSpec((1, n_exp), const)],
        out_shape=[jax.ShapeDtypeStruct((t, d), F32),
                   jax.ShapeDtypeStruct((t, TOP_K), jnp.int32),
                   jax.ShapeDtypeStruct((t, TOP_K), F32),
                   jax.ShapeDtypeStruct((t, TOP_K), jnp.int32),
                   jax.ShapeDtypeStruct((1, n_exp), F32)],
        scratch_shapes=[pltpu.VMEM((1, n_exp), F32)],
        compiler_params=_params("arbitrary"),
        name="router",
    )(x, g.reshape(1, d), w_router, b_router.reshape(1, n_exp))


def _expert_kernel(be_ref, nb_ref, tok_ref, h_hbm, w1_ref, b1_ref, w2_ref, b2_ref,
                   o_ref, xbuf, sem, *, tm):
    i = pl.program_id(0)
    nb = nb_ref[0]

    def issue(blk, slot):
        base = blk * tm

        def body(r, carry):
            tok = tok_ref[base + r]
            pltpu.make_async_copy(h_hbm.at[pl.ds(tok, 1), :], xbuf.at[slot, pl.ds(r, 1), :],
                                  sem.at[slot]).start()
            return carry

        lax.fori_loop(0, tm, body, 0)

    def wait(slot):
        pltpu.make_async_copy(h_hbm.at[pl.ds(0, tm), :], xbuf.at[slot], sem.at[slot]).wait()

    @pl.when(i == 0)
    def _():
        issue(0, 0)

    @pl.when(i < nb)
    def _():
        slot = i % 2
        wait(slot)
        xb = xbuf[slot].astype(BF16)
        other = 1 - slot
        base = (i + 1) * tm
        for r in range(tm):
            pltpu.make_async_copy(h_hbm.at[pl.ds(tok_ref[base + r], 1), :],
                                  xbuf.at[other, pl.ds(r, 1), :], sem.at[other]).start()
        hh = jnp.dot(xb, w1_ref[...], preferred_element_type=F32) + b1_ref[...]
        f = hh.shape[1] // 2
        lane = lax.broadcasted_iota(jnp.int32, (tm, LANES), 1)
        even = (lane & 1) == 0
        acts = []
        for c in range(f // LANES):
            lo = hh[:, c * LANES:(c + 1) * LANES]
            hi = hh[:, f + c * LANES:f + (c + 1) * LANES]
            g = jnp.where(even, lo, pltpu.roll(hi, 1, axis=1))
            li = jnp.where(even, pltpu.roll(lo, LANES - 1, axis=1), hi)
            glu = jnp.minimum(g, SWIGLU_LIMIT)
            lin = jnp.clip(li, -SWIGLU_LIMIT, SWIGLU_LIMIT)
            acts.append((glu * jax.nn.sigmoid(SWIGLU_ALPHA * glu) * (lin + 1.0)).astype(BF16))
        act = jnp.concatenate(acts, axis=1)
        y = jnp.dot(act, w2_ref[...], preferred_element_type=F32) + b2_ref[...]
        o_ref[...] = y

    @pl.when(i == nb)
    def _():
        wait(nb % 2)

    @pl.when(i >= nb)
    def _():
        o_ref[...] = jnp.zeros_like(o_ref)


def _experts(h, block_e, n_used, row_tok, w1, b1, w2, b2, tm):
    t, d = h.shape
    n_exp, f, _ = w2.shape
    assert f % (2 * LANES) == 0
    n_rows = row_tok.shape[0]
    n_blocks = n_rows // tm
    assert (t * TOP_K) % tm == 0 and n_blocks == (t * TOP_K) // tm + n_exp
    once = pl.Buffered(1)

    def wmap(i, be, nb, tok):
        return (be[i], 0, 0)

    def omap(i, be, nb, tok):
        return (i, 0)

    grid_spec = pltpu.PrefetchScalarGridSpec(
        num_scalar_prefetch=3,
        grid=(n_blocks,),
        in_specs=[pl.BlockSpec(memory_space=pl.ANY),
                  pl.BlockSpec((None, d, 2 * f), wmap, pipeline_mode=once),
                  pl.BlockSpec((None, 1, 2 * f), wmap),
                  pl.BlockSpec((None, f, d), wmap, pipeline_mode=once),
                  pl.BlockSpec((None, 1, d), wmap)],
        out_specs=pl.BlockSpec((tm, d), omap),
        scratch_shapes=[pltpu.VMEM((2, tm, d), F32),
                        pltpu.SemaphoreType.DMA((2,))],
    )
    return pl.pallas_call(
        functools.partial(_expert_kernel, tm=tm),
        grid_spec=grid_spec,
        out_shape=jax.ShapeDtypeStruct((n_rows, d), F32),
        compiler_params=_params("arbitrary"),
        name="experts",
    )(block_e, n_used, row_tok, h, w1, b1, w2, b2)


def _combine_kernel(pos_ref, x_ref, w_ref, g_ref, y_hbm, o_ref, ybuf, sem, *, tc, top_k, normalize):
    i = pl.program_id(0)
    n = pl.num_programs(0)

    def issue(blk, slot):
        base = blk * (tc * top_k)

        def body(r, carry):
            for k in range(top_k):
                p = pos_ref[base + r * top_k + k]
                pltpu.make_async_copy(y_hbm.at[pl.ds(p, 1), :], ybuf.at[slot, k, pl.ds(r, 1), :],
                                      sem.at[slot]).start()
            return carry

        lax.fori_loop(0, tc, body, 0)

    @pl.when(i == 0)
    def _():
        issue(0, 0)

    @pl.when(i + 1 < n)
    def _():
        other = (i + 1) % 2
        base = (i + 1) * (tc * top_k)
        for r in range(tc):
            for k in range(top_k):
                pltpu.make_async_copy(y_hbm.at[pl.ds(pos_ref[base + r * top_k + k], 1), :],
                                      ybuf.at[other, k, pl.ds(r, 1), :], sem.at[other]).start()

    slot = i % 2
    for k in range(top_k):
        pltpu.make_async_copy(y_hbm.at[pl.ds(0, tc), :], ybuf.at[slot, k], sem.at[slot]).wait()
    w = w_ref[...]
    acc = x_ref[...]
    for k in range(top_k):
        acc = acc + ybuf[slot, k] * w[:, k:k + 1]
    if normalize:
        ms = jnp.mean(acc * acc, axis=-1, keepdims=True)
        acc = acc * lax.rsqrt(ms + NORM_EPS) * g_ref[...]
    o_ref[...] = acc


def _combine(x, top_w, pos, y_rows, g, normalize, tc=128):
    t, d = x.shape
    tc = _tile(t, tc)
    grid_spec = pltpu.PrefetchScalarGridSpec(
        num_scalar_prefetch=1,
        grid=(t // tc,),
        in_specs=[pl.BlockSpec((tc, d), lambda i, p: (i, 0)),
                  pl.BlockSpec((tc, TOP_K), lambda i, p: (i, 0)),
                  pl.BlockSpec((1, d), lambda i, p: (0, 0)),
                  pl.BlockSpec(memory_space=pl.ANY)],
        out_specs=pl.BlockSpec((tc, d), lambda i, p: (i, 0)),
        scratch_shapes=[pltpu.VMEM((2, TOP_K, tc, d), F32),
                        pltpu.SemaphoreType.DMA((2,))],
    )
    return pl.pallas_call(
        functools.partial(_combine_kernel, tc=tc, top_k=TOP_K, normalize=normalize),
        grid_spec=grid_spec,
        out_shape=jax.ShapeDtypeStruct((t, d), F32),
        compiler_params=_params("arbitrary"),
        name="combine",
    )(pos.reshape(-1), x, top_w, g.reshape(1, d), y_rows)


def _moe_block_rows(n_assign):
    return 256 if n_assign >= 256 * 32 else 128


def kernel(x, mem, positions, norm_mix_g, w_in, w_s, b_s, ln_v_g, ln_v_b, beta_a, beta_b, w_out,
           norm_x_g, norm_mem_g, w_xq, w_xkv, w_xo, norm_ffn_g, w_router, b_router, w1, b1, w2, b2,
           norm_final_g):
    bsz, s, d = x.shape
    t = bsz * s
    depth = w_in.shape[0]
    chunk = w_s.shape[2]
    n_exp = w_router.shape[2]
    xs = x.reshape(t, d)
    mem2 = mem.reshape(-1, d)

    for l in range(depth):
        h = _rmsnorm(xs, norm_mix_g[l], BF16)
        proj = _matmul([h], w_in[l], BF16, name="w_in")
        out_a = _mixer_a(proj, w_s[l], b_s[l], ln_v_g[l], ln_v_b[l], beta_a[l])
        out_b = _mixer_b(proj, positions, beta_b[l], chunk, B_HEADS)
        xs = _matmul([out_a, out_b], w_out[l], F32, res=xs, name="w_out")

        h = _rmsnorm(xs, norm_x_g[l], BF16)
        mem_n = _rmsnorm(mem2, norm_mem_g[l], BF16)
        q = _matmul([h], w_xq[l], BF16, name="w_xq")
        kv = _matmul([mem_n], w_xkv[l], BF16, name="w_xkv")
        o = _xattn(q, kv, bsz, X_HEADS)
        xs = _matmul([o], w_xo[l], F32, res=xs, name="w_xo")

        hf, top_idx, top_w, rank, counts = _router(xs, norm_ffn_g[l], w_router[l], b_router[l])
        n_assign = t * TOP_K
        tm = _moe_block_rows(n_assign)
        counts = counts.reshape(n_exp).astype(jnp.int32)
        padded = (counts + tm - 1) // tm * tm
        pends = jnp.cumsum(padded)
        pstarts = pends - padded
        pos = pstarts[top_idx] + rank
        n_blocks = -(-n_assign // tm) + n_exp
        n_rows = n_blocks * tm
        row_tok = jnp.zeros((n_rows,), jnp.int32).at[pos.reshape(-1)].set(
            jnp.arange(n_assign, dtype=jnp.int32) // TOP_K)
        block_start = jnp.arange(n_blocks, dtype=jnp.int32) * tm
        block_e = jnp.sum((pends[None, :] <= block_start[:, None]).astype(jnp.int32), axis=1)
        block_e = jnp.minimum(block_e, n_exp - 1)
        n_used = (pends[-1:] // tm).astype(jnp.int32)
        f = w2.shape[2]
        w2p = w2[l].reshape(n_exp, 2, f // LANES, LANES // 2, d).transpose(0, 2, 3, 1, 4)
        w2p = w2p.reshape(n_exp, f, d).astype(BF16)
        y_rows = _experts(hf, block_e, n_used, row_tok, w1[l].astype(BF16),
                          b1[l].reshape(n_exp, 1, 2 * f), w2p, b2[l].reshape(n_exp, 1, d), tm)
        xs = _combine(xs, top_w, pos, y_rows, norm_final_g, normalize=(l + 1 == depth))
    return xs.reshape(bsz, s, d)
```

```python
import functools
import math

import jax
import jax.numpy as jnp
from jax import lax
from jax.experimental import pallas as pl
from jax.experimental.pallas import tpu as pltpu

B_HEADS = 8
X_HEADS = 4
TOP_K = 4
ROPE_BASE = 10000.0
NORM_EPS = 1e-5
RET_NORM_EPS = 1e-6
SWIGLU_LIMIT = 7.0
SWIGLU_ALPHA = 1.702

LANES = 128
GATHER_SLOTS = 3

V7X_VMEM_BYTES = 64 * 1024 * 1024
VMEM_LIMIT_BYTES = V7X_VMEM_BYTES - 6 * 1024 * 1024

F32 = jnp.float32
BF16 = jnp.bfloat16


def _params(*semantics):
    return pltpu.CompilerParams(
        dimension_semantics=semantics,
        vmem_limit_bytes=VMEM_LIMIT_BYTES,
        disable_bounds_checks=True,
    )


def _tile(n, want):
    t = min(n, want)
    while n % t:
        t //= 2
    return t


def _pack_pairs(v):
    half = v.shape[1] // 2
    bits = lax.bitcast_convert_type(v.astype(BF16).astype(F32), jnp.uint32)
    return (bits[:, :half] >> 16) | (bits[:, half:] & jnp.uint32(0xFFFF0000))


def _unpack_pairs(p):
    lo = lax.bitcast_convert_type(p << 16, F32)
    hi = lax.bitcast_convert_type(p & jnp.uint32(0xFFFF0000), F32)
    return lo, hi


def _rms_kernel(x_ref, g_ref, o_ref):
    x = x_ref[...]
    ms = jnp.mean(x * x, axis=-1, keepdims=True)
    o_ref[...] = (x * lax.rsqrt(ms + NORM_EPS) * g_ref[...]).astype(o_ref.dtype)


def _rmsnorm(x, g, out_dtype, tm=512):
    m, d = x.shape
    tm = _tile(m, tm)
    return pl.pallas_call(
        _rms_kernel,
        grid=(m // tm,),
        in_specs=[pl.BlockSpec((tm, d), lambda i: (i, 0)),
                  pl.BlockSpec((1, d), lambda i: (0, 0))],
        out_specs=pl.BlockSpec((tm, d), lambda i: (i, 0)),
        out_shape=jax.ShapeDtypeStruct((m, d), out_dtype),
        compiler_params=_params("arbitrary"),
        name="rmsnorm",
    )(x, g.reshape(1, d))


def _mm_kernel(*refs, n_a, has_res):
    a_refs = refs[:n_a]
    w_ref = refs[n_a]
    r_ref = refs[n_a + 1] if has_res else None
    o_ref, wbf = refs[-2:]

    @pl.when(pl.program_id(1) == 0)
    def _():
        wbf[...] = w_ref[...].astype(BF16)

    acc = None
    k0 = 0
    for a_ref in a_refs:
        ka = a_ref.shape[1]
        part = jnp.dot(a_ref[...], wbf[k0:k0 + ka, :], preferred_element_type=F32)
        acc = part if acc is None else acc + part
        k0 += ka
    if has_res:
        acc = acc + r_ref[...]
    o_ref[...] = acc.astype(o_ref.dtype)


def _matmul(a_parts, w, out_dtype, res=None, tm=1024, tn=512, name="matmul"):
    m = a_parts[0].shape[0]
    k, n = w.shape
    assert sum(a.shape[1] for a in a_parts) == k
    tm = _tile(m, tm)
    tn = _tile(n, tn)
    in_specs = [pl.BlockSpec((tm, a.shape[1]), lambda j, i: (i, 0)) for a in a_parts]
    in_specs.append(pl.BlockSpec((k, tn), lambda j, i: (0, j)))
    args = list(a_parts) + [w]
    if res is not None:
        in_specs.append(pl.BlockSpec((tm, tn), lambda j, i: (i, j)))
        args.append(res)
    return pl.pallas_call(
        functools.partial(_mm_kernel, n_a=len(a_parts), has_res=res is not None),
        grid=(n // tn, m // tm),
        in_specs=in_specs,
        out_specs=pl.BlockSpec((tm, tn), lambda j, i: (i, j)),
        out_shape=jax.ShapeDtypeStruct((m, n), out_dtype),
        scratch_shapes=[pltpu.VMEM((k, tn), BF16)],
        compiler_params=_params("arbitrary", "arbitrary"),
        name=name,
    )(*args)


def _amix_kernel(a_ref, ws_ref, bst_ref, lng_ref, lnb_ref, beta_ref, o_ref, *, chunk, heads, hd):
    aw = heads * hd
    a = a_ref[...].astype(F32)
    hid = 0.5 * a * (1.0 + lax.erf(a * (2.0 ** -0.5)))
    u = hid[:, :aw]
    v = hid[:, aw:]
    mean = jnp.mean(v, axis=-1, keepdims=True)
    vc = v - mean
    var = jnp.mean(vc * vc, axis=-1, keepdims=True)
    vn = (vc * lax.rsqrt(var + NORM_EPS) * lng_ref[...] + lnb_ref[...]).astype(BF16)
    ub = u * beta_ref[...]
    rows = a.shape[0]
    row = lax.broadcasted_iota(jnp.int32, (chunk, chunk), 0)
    col = lax.broadcasted_iota(jnp.int32, (chunk, chunk), 1)
    causal = row >= col
    bst = bst_ref[...]
    for h in range(heads):
        w = jnp.where(causal, ws_ref[h], 0.0).astype(BF16)
        bias = bst[:, h:h + 1]
        for c in range(rows // chunk):
            rs = slice(c * chunk, (c + 1) * chunk)
            cs = slice(h * hd, (h + 1) * hd)
            mixed = jnp.dot(w, vn[rs, cs], preferred_element_type=F32) + bias
            o_ref[rs, cs] = (ub[rs, cs] * mixed).astype(o_ref.dtype)


def _mixer_a(proj, w_s, b_s, ln_g, ln_b, beta, rows=256):
    t = proj.shape[0]
    heads, chunk, _ = w_s.shape
    aw = ln_g.shape[0]
    hd = aw // heads
    rows = _tile(t, rows)
    assert rows % chunk == 0
    kern = functools.partial(_amix_kernel, chunk=chunk, heads=heads, hd=hd)
    return pl.pallas_call(
        kern,
        grid=(t // rows,),
        in_specs=[pl.BlockSpec((rows, 2 * aw), lambda i: (i, 0)),
                  pl.BlockSpec((heads, chunk, chunk), lambda i: (0, 0, 0)),
                  pl.BlockSpec((chunk, heads), lambda i: (0, 0)),
                  pl.BlockSpec((1, aw), lambda i: (0, 0)),
                  pl.BlockSpec((1, aw), lambda i: (0, 0)),
                  pl.BlockSpec((1, aw), lambda i: (0, 0))],
        out_specs=pl.BlockSpec((rows, aw), lambda i: (i, 0)),
        out_shape=jax.ShapeDtypeStruct((t, aw), BF16),
        compiler_params=_params("arbitrary"),
        name="mixer_a",
    )(proj, w_s, b_s.T, ln_g.reshape(1, aw), ln_b.reshape(1, aw), beta.reshape(1, aw))


def _ret_kernel(pos_ref, invf_ref, din_ref, qd_ref, kd_ref, q_ref, k_ref, v_ref, g_ref, beta_ref,
                o_ref, state, *, heads, hd, chunk):
    @pl.when(pl.program_id(1) == 0)
    def _():
        state[...] = jnp.zeros_like(state)

    half = hd // 2
    ang = pos_ref[...] * invf_ref[...]
    cos = jnp.cos(ang)
    sin = jnp.sin(ang)
    qd_all = qd_ref[...]
    kd_all = kd_ref[...]
    k_scale = hd ** -0.5

    def rot(t):
        t1 = t[:, :half]
        t2 = t[:, half:]
        return jnp.concatenate([t1 * cos - t2 * sin, t2 * cos + t1 * sin], axis=-1)

    for h in range(heads):
        cs = slice(h * hd, (h + 1) * hd)
        chunk_decay = math.exp(math.log(1.0 - 2.0 ** (-5.0 - h)) * chunk)
        qr = rot(q_ref[:, cs].astype(F32))
        kr = rot(k_ref[:, cs].astype(F32)) * k_scale
        vb = v_ref[:, cs]
        scores = lax.dot_general(qr.astype(BF16), kr.astype(BF16), (((1,), (1,)), ((), ())),
                                 preferred_element_type=F32) * din_ref[h]
        inner = jnp.dot(scores.astype(BF16), vb, preferred_element_type=F32)
        st = state[h]
        cross = jnp.dot((qr * qd_all[:, h:h + 1]).astype(BF16), st.astype(BF16),
                        preferred_element_type=F32)
        kv = lax.dot_general((kr * kd_all[:, h:h + 1]).astype(BF16), vb, (((0,), (0,)), ((), ())),
                             preferred_element_type=F32)
        state[h] = st * chunk_decay + kv
        out = inner + cross
        out = out * lax.rsqrt(jnp.mean(out * out, axis=-1, keepdims=True) + RET_NORM_EPS)
        g = g_ref[:, cs].astype(F32)
        out = out * (g * jax.nn.sigmoid(g)) * beta_ref[:, cs]
        o_ref[:, cs] = out.astype(o_ref.dtype)


def _mixer_b(proj, positions, beta, chunk, heads):
    t = proj.shape[0]
    bsz, s = positions.shape
    bw = beta.shape[0]
    hd = bw // heads
    half = hd // 2
    n_chunks = s // chunk
    assert proj.shape[1] == 6 * bw
    log_gamma = jnp.log(1.0 - 2.0 ** (-5.0 - jnp.arange(heads, dtype=F32)))
    idx = jnp.arange(chunk, dtype=F32)
    diff = idx[:, None] - idx[None, :]
    decay_in = jnp.where(diff >= 0, jnp.exp(log_gamma[:, None, None] * jnp.maximum(diff, 0.0)), 0.0)
    q_decay = jnp.exp(log_gamma[None, :] * (idx[:, None] + 1.0))
    k_decay = jnp.exp(log_gamma[None, :] * (chunk - 1.0 - idx[:, None]))
    inv_freq = (ROPE_BASE ** (-jnp.arange(half, dtype=F32) / half)).reshape(1, half)
    pos = positions.astype(F32).reshape(t, 1)

    kern = functools.partial(_ret_kernel, heads=heads, hd=hd, chunk=chunk)
    row_map = lambda b, c: (b * n_chunks + c, 0)
    const2 = lambda b, c: (0, 0)

    def col_block(jb):
        return pl.BlockSpec((chunk, bw), lambda b, c: (b * n_chunks + c, jb))

    return pl.pallas_call(
        kern,
        grid=(bsz, n_chunks),
        in_specs=[pl.BlockSpec((chunk, 1), row_map),
                  pl.BlockSpec((1, half), const2),
                  pl.BlockSpec((heads, chunk, chunk), lambda b, c: (0, 0, 0)),
                  pl.BlockSpec((chunk, heads), const2),
                  pl.BlockSpec((chunk, heads), const2),
                  col_block(2), col_block(3), col_block(4), col_block(5),
                  pl.BlockSpec((1, bw), const2)],
        out_specs=pl.BlockSpec((chunk, bw), row_map),
        out_shape=jax.ShapeDtypeStruct((t, bw), BF16),
        scratch_shapes=[pltpu.VMEM((heads, hd, hd), F32)],
        compiler_params=_params("arbitrary", "arbitrary"),
        name="mixer_b",
    )(pos, inv_freq, decay_in, q_decay, k_decay, proj, proj, proj, proj, beta.reshape(1, bw))


def _xattn_kernel(q_ref, k_ref, v_ref, o_ref, *, scale):
    s = lax.dot_general(q_ref[...], k_ref[...], (((1,), (1,)), ((), ())),
                        preferred_element_type=F32) * scale
    m = jnp.max(s, axis=-1, keepdims=True)
    p = jnp.exp(s - m)
    l = jnp.sum(p, axis=-1, keepdims=True)
    o = jnp.dot(p.astype(BF16), v_ref[...], preferred_element_type=F32)
    o_ref[...] = (o / l).astype(o_ref.dtype)


def _xattn(q, kv, bsz, heads, tq=512):
    t, d = q.shape
    s = t // bsz
    m = kv.shape[0] // bsz
    dh = d // heads
    tq = _tile(s, tq)
    nq = s // tq
    return pl.pallas_call(
        functools.partial(_xattn_kernel, scale=dh ** -0.5),
        grid=(bsz, nq, heads),
        in_specs=[pl.BlockSpec((tq, dh), lambda b, i, h: (b * nq + i, h)),
                  pl.BlockSpec((m, dh), lambda b, i, h: (b, h)),
                  pl.BlockSpec((m, dh), lambda b, i, h: (b, heads + h))],
        out_specs=pl.BlockSpec((tq, dh), lambda b, i, h: (b * nq + i, h)),
        out_shape=jax.ShapeDtypeStruct((t, d), BF16),
        compiler_params=_params("arbitrary", "arbitrary", "arbitrary"),
        name="xattn",
    )(q, kv, kv)


def _router_kernel(x_ref, g_ref, wr_ref, br_ref, h_ref, idx_ref, w_ref, rank_ref, cnt_ref, cnt, *,
                   n_exp, top_k):
    @pl.when(pl.program_id(0) == 0)
    def _():
        cnt[...] = jnp.zeros_like(cnt)

    x = x_ref[...]
    ms = jnp.mean(x * x, axis=-1, keepdims=True)
    h = x * lax.rsqrt(ms + NORM_EPS) * g_ref[...]
    h_ref[...] = _pack_pairs(h)
    wr = wr_ref[...]
    h_hi = h.astype(BF16)
    h_lo = (h - h_hi.astype(F32)).astype(BF16)
    w_hi = wr.astype(BF16)
    w_lo = (wr - w_hi.astype(F32)).astype(BF16)
    logits = (jnp.dot(h_hi, w_hi, preferred_element_type=F32)
              + jnp.dot(h_hi, w_lo, preferred_element_type=F32)
              + jnp.dot(h_lo, w_hi, preferred_element_type=F32)) + br_ref[...]
    tt = x.shape[0]
    lane = lax.broadcasted_iota(jnp.int32, (tt, n_exp), 1)
    work = logits
    vals, idxs = [], []
    onehot = jnp.zeros((tt, n_exp), F32)
    for _ in range(top_k):
        mx = jnp.max(work, axis=-1, keepdims=True)
        ix = jnp.min(jnp.where(work == mx, lane, n_exp), axis=-1, keepdims=True)
        sel = lane == ix
        onehot = jnp.where(sel, 1.0, onehot)
        work = jnp.where(sel, -jnp.inf, work)
        vals.append(mx)
        idxs.append(ix)
    exps = [jnp.exp(v - vals[0]) for v in vals]
    denom = exps[0]
    for e in exps[1:]:
        denom = denom + e
    r = lax.broadcasted_iota(jnp.int32, (tt, tt), 0)
    c = lax.broadcasted_iota(jnp.int32, (tt, tt), 1)
    lower = jnp.where(r > c, 1.0, 0.0).astype(BF16)
    before = jnp.dot(lower, onehot.astype(BF16), preferred_element_type=F32) + cnt[...]
    for k in range(top_k):
        idx_ref[:, k:k + 1] = idxs[k]
        w_ref[:, k:k + 1] = exps[k] / denom
        rk = jnp.sum(jnp.where(lane == idxs[k], before, 0.0), axis=-1, keepdims=True)
        rank_ref[:, k:k + 1] = rk.astype(jnp.int32)
    cnt[...] = cnt[...] + jnp.sum(onehot, axis=0, keepdims=True)
    cnt_ref[...] = cnt[...]


def _router(x, g, w_router, b_router, tt=256):
    t, d = x.shape
    n_exp = w_router.shape[1]
    tt = _tile(t, tt)
    kern = functools.partial(_router_kernel, n_exp=n_exp, top_k=TOP_K)
    tok_map = lambda i: (i, 0)
    const = lambda i: (0, 0)
    return pl.pallas_call(
        kern,
        grid=(t // tt,),
        in_specs=[pl.BlockSpec((tt, d), tok_map),
                  pl.BlockSpec((1, d), const),
                  pl.BlockSpec((d, n_exp), const),
                  pl.BlockSpec((1, n_exp), const)],
        out_specs=[pl.BlockSpec((tt, d // 2), tok_map),
                   pl.BlockSpec((tt, TOP_K), tok_map),
                   pl.BlockSpec((tt, TOP_K), tok_map),
                   pl.BlockSpec((tt, TOP_K), tok_map),
                   pl.BlockSpec((1, n_exp), const)],
        out_shape=[jax.ShapeDtypeStruct((t, d // 2), jnp.uint32),
                   jax.ShapeDtypeStruct((t, TOP_K), jnp.int32),
                   jax.ShapeDtypeStruct((t, TOP_K), F32),
                   jax.ShapeDtypeStruct((t, TOP_K), jnp.int32),
                   jax.ShapeDtypeStruct((1, n_exp), F32)],
        scratch_shapes=[pltpu.VMEM((1, n_exp), F32)],
        compiler_params=_params("arbitrary"),
        name="router",
    )(x, g.reshape(1, d), w_router, b_router.reshape(1, n_exp))


def _expert_kernel(be_ref, nb_ref, tok_ref, h_hbm, w1_ref, b1_ref, w2_ref, b2_ref,
                   o_ref, xbuf, sem, *, tm):
    i = pl.program_id(0)
    nb = nb_ref[0]

    def issue(blk, slot):
        base = blk * tm

        def body(r, carry):
            tok = tok_ref[base + r]
            pltpu.make_async_copy(h_hbm.at[pl.ds(tok, 1), :], xbuf.at[slot, pl.ds(r, 1), :],
                                  sem.at[slot]).start()
            return carry

        lax.fori_loop(0, tm, body, 0)

    def wait(slot):
        pltpu.make_async_copy(h_hbm.at[pl.ds(0, tm), :], xbuf.at[slot], sem.at[slot]).wait()

    @pl.when(i == 0)
    def _():
        issue(0, 0)
        issue(1, 1)

    @pl.when(i < nb)
    def _():
        wait(i % GATHER_SLOTS)
        lo, hi = _unpack_pairs(xbuf[i % GATHER_SLOTS])
        ahead = (i + 2) % GATHER_SLOTS
        base = (i + 2) * tm
        for r in range(tm):
            pltpu.make_async_copy(h_hbm.at[pl.ds(tok_ref[base + r], 1), :],
                                  xbuf.at[ahead, pl.ds(r, 1), :], sem.at[ahead]).start()
        half = lo.shape[1]
        hh = (jnp.dot(lo.astype(BF16), w1_ref[:half, :], preferred_element_type=F32)
              + jnp.dot(hi.astype(BF16), w1_ref[half:, :], preferred_element_type=F32)
              + b1_ref[...])
        f = hh.shape[1] // 2
        lane = lax.broadcasted_iota(jnp.int32, (tm, LANES), 1)
        first = lane < LANES // 2
        idx_glu = (2 * lane) % LANES
        idx_lin = (2 * lane + 1) % LANES
        acts = []
        for c in range(f // LANES):
            v0 = hh[:, 2 * c * LANES:(2 * c + 1) * LANES]
            v1 = hh[:, (2 * c + 1) * LANES:(2 * c + 2) * LANES]
            g = jnp.where(first, jnp.take_along_axis(v0, idx_glu, axis=1),
                          jnp.take_along_axis(v1, idx_glu, axis=1))
            li = jnp.where(first, jnp.take_along_axis(v0, idx_lin, axis=1),
                           jnp.take_along_axis(v1, idx_lin, axis=1))
            glu = jnp.minimum(g, SWIGLU_LIMIT)
            lin = jnp.clip(li, -SWIGLU_LIMIT, SWIGLU_LIMIT)
            acts.append((glu * jax.nn.sigmoid(SWIGLU_ALPHA * glu) * (lin + 1.0)).astype(BF16))
        act = jnp.concatenate(acts, axis=1)
        y = jnp.dot(act, w2_ref[...], preferred_element_type=F32) + b2_ref[...]
        o_ref[...] = _pack_pairs(y)

    @pl.when(i == nb)
    def _():
        wait(nb % GATHER_SLOTS)
        wait((nb + 1) % GATHER_SLOTS)

    @pl.when(i >= nb)
    def _():
        o_ref[...] = jnp.zeros_like(o_ref)


def _experts(hp, block_e, n_used, row_tok, w1, b1, w2, b2, tm):
    t, half = hp.shape
    n_exp, f, d = w2.shape
    assert f % (2 * LANES) == 0 and d == 2 * half
    n_blocks = row_tok.shape[0] // tm - 1
    n_rows = n_blocks * tm
    assert (t * TOP_K) % tm == 0 and n_blocks == (t * TOP_K) // tm + n_exp
    once = pl.Buffered(1)

    def wmap(i, be, nb, tok):
        return (be[i], 0, 0)

    def omap(i, be, nb, tok):
        return (i, 0)

    grid_spec = pltpu.PrefetchScalarGridSpec(
        num_scalar_prefetch=3,
        grid=(n_blocks,),
        in_specs=[pl.BlockSpec(memory_space=pl.ANY),
                  pl.BlockSpec((None, d, 2 * f), wmap, pipeline_mode=once),
                  pl.BlockSpec((None, 1, 2 * f), wmap),
                  pl.BlockSpec((None, f, d), wmap, pipeline_mode=once),
                  pl.BlockSpec((None, 1, d), wmap)],
        out_specs=pl.BlockSpec((tm, half), omap),
        scratch_shapes=[pltpu.VMEM((GATHER_SLOTS, tm, half), jnp.uint32),
                        pltpu.SemaphoreType.DMA((GATHER_SLOTS,))],
    )
    return pl.pallas_call(
        functools.partial(_expert_kernel, tm=tm),
        grid_spec=grid_spec,
        out_shape=jax.ShapeDtypeStruct((n_rows, half), jnp.uint32),
        compiler_params=_params("arbitrary"),
        name="experts",
    )(block_e, n_used, row_tok, hp, w1, b1, w2, b2)


def _combine_kernel(pos_ref, x_ref, w_ref, g_ref, y_hbm, o_ref, ybuf, sem, *, tc, top_k, normalize):
    i = pl.program_id(0)
    n = pl.num_programs(0)

    def issue(blk, slot):
        base = blk * (tc * top_k)

        def body(r, carry):
            for k in range(top_k):
                p = pos_ref[base + r * top_k + k]
                pltpu.make_async_copy(y_hbm.at[pl.ds(p, 1), :], ybuf.at[slot, k, pl.ds(r, 1), :],
                                      sem.at[slot]).start()
            return carry

        lax.fori_loop(0, tc, body, 0)

    def wait(slot):
        for k in range(top_k):
            pltpu.make_async_copy(y_hbm.at[pl.ds(0, tc), :], ybuf.at[slot, k], sem.at[slot]).wait()

    @pl.when(i == 0)
    def _():
        issue(0, 0)
        issue(1, 1)

    slot = i % GATHER_SLOTS
    wait(slot)
    w = w_ref[...]
    x = x_ref[...]
    half = x.shape[1] // 2
    acc_lo = x[:, :half]
    acc_hi = x[:, half:]
    for k in range(top_k):
        lo, hi = _unpack_pairs(ybuf[slot, k])
        acc_lo = acc_lo + lo * w[:, k:k + 1]
        acc_hi = acc_hi + hi * w[:, k:k + 1]
    acc = jnp.concatenate([acc_lo, acc_hi], axis=1)
    if normalize:
        ms = jnp.mean(acc * acc, axis=-1, keepdims=True)
        acc = acc * lax.rsqrt(ms + NORM_EPS) * g_ref[...]
    o_ref[...] = acc
    ahead = (i + 2) % GATHER_SLOTS
    base = (i + 2) * (tc * top_k)
    for r in range(tc):
        for k in range(top_k):
            pltpu.make_async_copy(y_hbm.at[pl.ds(pos_ref[base + r * top_k + k], 1), :],
                                  ybuf.at[ahead, k, pl.ds(r, 1), :], sem.at[ahead]).start()

    @pl.when(i == n - 1)
    def _():
        wait((n % GATHER_SLOTS))
        wait(((n + 1) % GATHER_SLOTS))


def _combine(x, top_w, pos, y_rows, g, normalize, tc=128):
    t, d = x.shape
    tc = _tile(t, tc)
    assert t // tc >= 2
    pos = jnp.concatenate([pos.reshape(-1), jnp.zeros((2 * tc * TOP_K,), jnp.int32)])
    grid_spec = pltpu.PrefetchScalarGridSpec(
        num_scalar_prefetch=1,
        grid=(t // tc,),
        in_specs=[pl.BlockSpec((tc, d), lambda i, p: (i, 0)),
                  pl.BlockSpec((tc, TOP_K), lambda i, p: (i, 0)),
                  pl.BlockSpec((1, d), lambda i, p: (0, 0)),
                  pl.BlockSpec(memory_space=pl.ANY)],
        out_specs=pl.BlockSpec((tc, d), lambda i, p: (i, 0)),
        scratch_shapes=[pltpu.VMEM((GATHER_SLOTS, TOP_K, tc, d // 2), jnp.uint32),
                        pltpu.SemaphoreType.DMA((GATHER_SLOTS,))],
    )
    return pl.pallas_call(
        functools.partial(_combine_kernel, tc=tc, top_k=TOP_K, normalize=normalize),
        grid_spec=grid_spec,
        out_shape=jax.ShapeDtypeStruct((t, d), F32),
        compiler_params=_params("arbitrary"),
        name="combine",
    )(pos, x, top_w, g.reshape(1, d), y_rows)


def _moe_block_rows(n_assign):
    return 256 if n_assign >= 256 * 32 else 128


def kernel(x, mem, positions, norm_mix_g, w_in, w_s, b_s, ln_v_g, ln_v_b, beta_a, beta_b, w_out,
           norm_x_g, norm_mem_g, w_xq, w_xkv, w_xo, norm_ffn_g, w_router, b_router, w1, b1, w2, b2,
           norm_final_g):
    bsz, s, d = x.shape
    t = bsz * s
    depth = w_in.shape[0]
    chunk = w_s.shape[2]
    n_exp = w_router.shape[2]
    xs = x.reshape(t, d)
    mem2 = mem.reshape(-1, d)

    for l in range(depth):
        h = _rmsnorm(xs, norm_mix_g[l], BF16)
        proj = _matmul([h], w_in[l], BF16, name="w_in")
        out_a = _mixer_a(proj, w_s[l], b_s[l], ln_v_g[l], ln_v_b[l], beta_a[l])
        out_b = _mixer_b(proj, positions, beta_b[l], chunk, B_HEADS)
        xs = _matmul([out_a, out_b], w_out[l], F32, res=xs, name="w_out")

        h = _rmsnorm(xs, norm_x_g[l], BF16)
        mem_n = _rmsnorm(mem2, norm_mem_g[l], BF16)
        q = _matmul([h], w_xq[l], BF16, name="w_xq")
        kv = _matmul([mem_n], w_xkv[l], BF16, name="w_xkv")
        o = _xattn(q, kv, bsz, X_HEADS)
        xs = _matmul([o], w_xo[l], F32, res=xs, name="w_xo")

        hf, top_idx, top_w, rank, counts = _router(xs, norm_ffn_g[l], w_router[l], b_router[l])
        n_assign = t * TOP_K
        tm = _moe_block_rows(n_assign)
        counts = counts.reshape(n_exp).astype(jnp.int32)
        padded = (counts + tm - 1) // tm * tm
        pends = jnp.cumsum(padded)
        pstarts = pends - padded
        pos = pstarts[top_idx] + rank
        n_blocks = -(-n_assign // tm) + n_exp
        n_rows = n_blocks * tm
        row_tok = jnp.zeros((n_rows + tm,), jnp.int32).at[pos.reshape(-1)].set(
            jnp.arange(n_assign, dtype=jnp.int32) // TOP_K)
        block_start = jnp.arange(n_blocks, dtype=jnp.int32) * tm
        block_e = jnp.sum((pends[None, :] <= block_start[:, None]).astype(jnp.int32), axis=1)
        block_e = jnp.minimum(block_e, n_exp - 1)
        n_used = (pends[-1:] // tm).astype(jnp.int32)
        f = w2.shape[2]
        y_rows = _experts(hf, block_e, n_used, row_tok, w1[l].astype(BF16),
                          b1[l].reshape(n_exp, 1, 2 * f), w2[l].astype(BF16),
                          b2[l].reshape(n_exp, 1, d), tm)
        xs = _combine(xs, top_w, pos, y_rows, norm_final_g, normalize=(l + 1 == depth))
    return xs.reshape(bsz, s, d)
```

```python
import functools
import math

import jax
import jax.numpy as jnp
from jax import lax
from jax.experimental import pallas as pl
from jax.experimental.pallas import tpu as pltpu

B_HEADS = 8
X_HEADS = 4
TOP_K = 4
ROPE_BASE = 10000.0
NORM_EPS = 1e-5
RET_NORM_EPS = 1e-6
SWIGLU_LIMIT = 7.0
SWIGLU_ALPHA = 1.702

LANES = 128
GATHER_SLOTS = 3
W1_SLAB_ROWS = 512
W2_SLAB_ROWS = 256

V7X_VMEM_BYTES = 64 * 1024 * 1024
VMEM_LIMIT_BYTES = V7X_VMEM_BYTES - 6 * 1024 * 1024

F32 = jnp.float32
BF16 = jnp.bfloat16


def _params(*semantics):
    return pltpu.CompilerParams(
        dimension_semantics=semantics,
        vmem_limit_bytes=VMEM_LIMIT_BYTES,
        disable_bounds_checks=True,
    )


def _tile(n, want):
    t = min(n, want)
    while n % t:
        t //= 2
    return t


def _pack_pairs(v):
    half = v.shape[1] // 2
    bits = lax.bitcast_convert_type(v.astype(BF16).astype(F32), jnp.uint32)
    return (bits[:, :half] >> 16) | (bits[:, half:] & jnp.uint32(0xFFFF0000))


def _unpack_pairs(p):
    lo = lax.bitcast_convert_type(p << 16, F32)
    hi = lax.bitcast_convert_type(p & jnp.uint32(0xFFFF0000), F32)
    return lo, hi


def _rms_kernel(x_ref, g_ref, o_ref):
    x = x_ref[...]
    ms = jnp.mean(x * x, axis=-1, keepdims=True)
    o_ref[...] = (x * lax.rsqrt(ms + NORM_EPS) * g_ref[...]).astype(o_ref.dtype)


def _rmsnorm(x, g, out_dtype, tm=512):
    m, d = x.shape
    tm = _tile(m, tm)
    return pl.pallas_call(
        _rms_kernel,
        grid=(m // tm,),
        in_specs=[pl.BlockSpec((tm, d), lambda i: (i, 0)),
                  pl.BlockSpec((1, d), lambda i: (0, 0))],
        out_specs=pl.BlockSpec((tm, d), lambda i: (i, 0)),
        out_shape=jax.ShapeDtypeStruct((m, d), out_dtype),
        compiler_params=_params("arbitrary"),
        name="rmsnorm",
    )(x, g.reshape(1, d))


def _mm_kernel(*refs, n_a, has_res):
    a_refs = refs[:n_a]
    w_ref = refs[n_a]
    r_ref = refs[n_a + 1] if has_res else None
    o_ref, wbf = refs[-2:]

    @pl.when(pl.program_id(1) == 0)
    def _():
        wbf[...] = w_ref[...].astype(BF16)

    acc = None
    k0 = 0
    for a_ref in a_refs:
        ka = a_ref.shape[1]
        part = jnp.dot(a_ref[...], wbf[k0:k0 + ka, :], preferred_element_type=F32)
        acc = part if acc is None else acc + part
        k0 += ka
    if has_res:
        acc = acc + r_ref[...]
    o_ref[...] = acc.astype(o_ref.dtype)


def _matmul(a_parts, w, out_dtype, res=None, tm=1024, tn=512, name="matmul"):
    m = a_parts[0].shape[0]
    k, n = w.shape
    assert sum(a.shape[1] for a in a_parts) == k
    tm = _tile(m, tm)
    tn = _tile(n, tn)
    in_specs = [pl.BlockSpec((tm, a.shape[1]), lambda j, i: (i, 0)) for a in a_parts]
    in_specs.append(pl.BlockSpec((k, tn), lambda j, i: (0, j)))
    args = list(a_parts) + [w]
    if res is not None:
        in_specs.append(pl.BlockSpec((tm, tn), lambda j, i: (i, j)))
        args.append(res)
    return pl.pallas_call(
        functools.partial(_mm_kernel, n_a=len(a_parts), has_res=res is not None),
        grid=(n // tn, m // tm),
        in_specs=in_specs,
        out_specs=pl.BlockSpec((tm, tn), lambda j, i: (i, j)),
        out_shape=jax.ShapeDtypeStruct((m, n), out_dtype),
        scratch_shapes=[pltpu.VMEM((k, tn), BF16)],
        compiler_params=_params("arbitrary", "arbitrary"),
        name=name,
    )(*args)


def _amix_kernel(a_ref, ws_ref, bst_ref, lng_ref, lnb_ref, beta_ref, o_ref, *, chunk, heads, hd):
    aw = heads * hd
    a = a_ref[...].astype(F32)
    hid = 0.5 * a * (1.0 + lax.erf(a * (2.0 ** -0.5)))
    u = hid[:, :aw]
    v = hid[:, aw:]
    mean = jnp.mean(v, axis=-1, keepdims=True)
    vc = v - mean
    var = jnp.mean(vc * vc, axis=-1, keepdims=True)
    vn = (vc * lax.rsqrt(var + NORM_EPS) * lng_ref[...] + lnb_ref[...]).astype(BF16)
    ub = u * beta_ref[...]
    rows = a.shape[0]
    row = lax.broadcasted_iota(jnp.int32, (chunk, chunk), 0)
    col = lax.broadcasted_iota(jnp.int32, (chunk, chunk), 1)
    causal = row >= col
    bst = bst_ref[...]
    for h in range(heads):
        w = jnp.where(causal, ws_ref[h], 0.0).astype(BF16)
        bias = bst[:, h:h + 1]
        for c in range(rows // chunk):
            rs = slice(c * chunk, (c + 1) * chunk)
            cs = slice(h * hd, (h + 1) * hd)
            mixed = jnp.dot(w, vn[rs, cs], preferred_element_type=F32) + bias
            o_ref[rs, cs] = (ub[rs, cs] * mixed).astype(o_ref.dtype)


def _mixer_a(proj, w_s, b_s, ln_g, ln_b, beta, rows=256):
    t = proj.shape[0]
    heads, chunk, _ = w_s.shape
    aw = ln_g.shape[0]
    hd = aw // heads
    rows = _tile(t, rows)
    assert rows % chunk == 0
    kern = functools.partial(_amix_kernel, chunk=chunk, heads=heads, hd=hd)
    return pl.pallas_call(
        kern,
        grid=(t // rows,),
        in_specs=[pl.BlockSpec((rows, 2 * aw), lambda i: (i, 0)),
                  pl.BlockSpec((heads, chunk, chunk), lambda i: (0, 0, 0)),
                  pl.BlockSpec((chunk, heads), lambda i: (0, 0)),
                  pl.BlockSpec((1, aw), lambda i: (0, 0)),
                  pl.BlockSpec((1, aw), lambda i: (0, 0)),
                  pl.BlockSpec((1, aw), lambda i: (0, 0))],
        out_specs=pl.BlockSpec((rows, aw), lambda i: (i, 0)),
        out_shape=jax.ShapeDtypeStruct((t, aw), BF16),
        compiler_params=_params("arbitrary"),
        name="mixer_a",
    )(proj, w_s, b_s.T, ln_g.reshape(1, aw), ln_b.reshape(1, aw), beta.reshape(1, aw))


def _ret_kernel(pos_ref, invf_ref, din_ref, qd_ref, kd_ref, q_ref, k_ref, v_ref, g_ref, beta_ref,
                o_ref, state, *, heads, hd, chunk):
    @pl.when(pl.program_id(1) == 0)
    def _():
        state[...] = jnp.zeros_like(state)

    half = hd // 2
    ang = pos_ref[...] * invf_ref[...]
    cos = jnp.cos(ang)
    sin = jnp.sin(ang)
    qd_all = qd_ref[...]
    kd_all = kd_ref[...]
    k_scale = hd ** -0.5

    def rot(t):
        t1 = t[:, :half]
        t2 = t[:, half:]
        return jnp.concatenate([t1 * cos - t2 * sin, t2 * cos + t1 * sin], axis=-1)

    for h in range(heads):
        cs = slice(h * hd, (h + 1) * hd)
        chunk_decay = math.exp(math.log(1.0 - 2.0 ** (-5.0 - h)) * chunk)
        qr = rot(q_ref[:, cs].astype(F32))
        kr = rot(k_ref[:, cs].astype(F32)) * k_scale
        vb = v_ref[:, cs]
        scores = lax.dot_general(qr.astype(BF16), kr.astype(BF16), (((1,), (1,)), ((), ())),
                                 preferred_element_type=F32) * din_ref[h]
        inner = jnp.dot(scores.astype(BF16), vb, preferred_element_type=F32)
        st = state[h]
        cross = jnp.dot((qr * qd_all[:, h:h + 1]).astype(BF16), st.astype(BF16),
                        preferred_element_type=F32)
        kv = lax.dot_general((kr * kd_all[:, h:h + 1]).astype(BF16), vb, (((0,), (0,)), ((), ())),
                             preferred_element_type=F32)
        state[h] = st * chunk_decay + kv
        out = inner + cross
        out = out * lax.rsqrt(jnp.mean(out * out, axis=-1, keepdims=True) + RET_NORM_EPS)
        g = g_ref[:, cs].astype(F32)
        out = out * (g * jax.nn.sigmoid(g)) * beta_ref[:, cs]
        o_ref[:, cs] = out.astype(o_ref.dtype)


def _mixer_b(proj, positions, beta, chunk, heads):
    t = proj.shape[0]
    bsz, s = positions.shape
    bw = beta.shape[0]
    hd = bw // heads
    half = hd // 2
    n_chunks = s // chunk
    assert proj.shape[1] == 6 * bw
    log_gamma = jnp.log(1.0 - 2.0 ** (-5.0 - jnp.arange(heads, dtype=F32)))
    idx = jnp.arange(chunk, dtype=F32)
    diff = idx[:, None] - idx[None, :]
    decay_in = jnp.where(diff >= 0, jnp.exp(log_gamma[:, None, None] * jnp.maximum(diff, 0.0)), 0.0)
    q_decay = jnp.exp(log_gamma[None, :] * (idx[:, None] + 1.0))
    k_decay = jnp.exp(log_gamma[None, :] * (chunk - 1.0 - idx[:, None]))
    inv_freq = (ROPE_BASE ** (-jnp.arange(half, dtype=F32) / half)).reshape(1, half)
    pos = positions.astype(F32).reshape(t, 1)

    kern = functools.partial(_ret_kernel, heads=heads, hd=hd, chunk=chunk)
    row_map = lambda b, c: (b * n_chunks + c, 0)
    const2 = lambda b, c: (0, 0)

    def col_block(jb):
        return pl.BlockSpec((chunk, bw), lambda b, c: (b * n_chunks + c, jb))

    return pl.pallas_call(
        kern,
        grid=(bsz, n_chunks),
        in_specs=[pl.BlockSpec((chunk, 1), row_map),
                  pl.BlockSpec((1, half), const2),
                  pl.BlockSpec((heads, chunk, chunk), lambda b, c: (0, 0, 0)),
                  pl.BlockSpec((chunk, heads), const2),
                  pl.BlockSpec((chunk, heads), const2),
                  col_block(2), col_block(3), col_block(4), col_block(5),
                  pl.BlockSpec((1, bw), const2)],
        out_specs=pl.BlockSpec((chunk, bw), row_map),
        out_shape=jax.ShapeDtypeStruct((t, bw), BF16),
        scratch_shapes=[pltpu.VMEM((heads, hd, hd), F32)],
        compiler_params=_params("arbitrary", "arbitrary"),
        name="mixer_b",
    )(pos, inv_freq, decay_in, q_decay, k_decay, proj, proj, proj, proj, beta.reshape(1, bw))


def _xattn_kernel(q_ref, k_ref, v_ref, o_ref, *, scale):
    s = lax.dot_general(q_ref[...], k_ref[...], (((1,), (1,)), ((), ())),
                        preferred_element_type=F32) * scale
    m = jnp.max(s, axis=-1, keepdims=True)
    p = jnp.exp(s - m)
    l = jnp.sum(p, axis=-1, keepdims=True)
    o = jnp.dot(p.astype(BF16), v_ref[...], preferred_element_type=F32)
    o_ref[...] = (o / l).astype(o_ref.dtype)


def _xattn(q, kv, bsz, heads, tq=512):
    t, d = q.shape
    s = t // bsz
    m = kv.shape[0] // bsz
    dh = d // heads
    tq = _tile(s, tq)
    nq = s // tq
    return pl.pallas_call(
        functools.partial(_xattn_kernel, scale=dh ** -0.5),
        grid=(bsz, nq, heads),
        in_specs=[pl.BlockSpec((tq, dh), lambda b, i, h: (b * nq + i, h)),
                  pl.BlockSpec((m, dh), lambda b, i, h: (b, h)),
                  pl.BlockSpec((m, dh), lambda b, i, h: (b, heads + h))],
        out_specs=pl.BlockSpec((tq, dh), lambda b, i, h: (b * nq + i, h)),
        out_shape=jax.ShapeDtypeStruct((t, d), BF16),
        compiler_params=_params("arbitrary", "arbitrary", "arbitrary"),
        name="xattn",
    )(q, kv, kv)


def _router_kernel(x_ref, g_ref, wr_ref, br_ref, h_ref, idx_ref, w_ref, rank_ref, cnt_ref, cnt, *,
                   n_exp, top_k):
    @pl.when(pl.program_id(0) == 0)
    def _():
        cnt[...] = jnp.zeros_like(cnt)

    x = x_ref[...]
    ms = jnp.mean(x * x, axis=-1, keepdims=True)
    h = x * lax.rsqrt(ms + NORM_EPS) * g_ref[...]
    h_ref[...] = _pack_pairs(h)
    wr = wr_ref[...]
    h_hi = h.astype(BF16)
    h_lo = (h - h_hi.astype(F32)).astype(BF16)
    w_hi = wr.astype(BF16)
    w_lo = (wr - w_hi.astype(F32)).astype(BF16)
    logits = (jnp.dot(h_hi, w_hi, preferred_element_type=F32)
              + jnp.dot(h_hi, w_lo, preferred_element_type=F32)
              + jnp.dot(h_lo, w_hi, preferred_element_type=F32)) + br_ref[...]
    tt = x.shape[0]
    lane = lax.broadcasted_iota(jnp.int32, (tt, n_exp), 1)
    work = logits
    vals, idxs = [], []
    onehot = jnp.zeros((tt, n_exp), F32)
    for _ in range(top_k):
        mx = jnp.max(work, axis=-1, keepdims=True)
        ix = jnp.min(jnp.where(work == mx, lane, n_exp), axis=-1, keepdims=True)
        sel = lane == ix
        onehot = jnp.where(sel, 1.0, onehot)
        work = jnp.where(sel, -jnp.inf, work)
        vals.append(mx)
        idxs.append(ix)
    exps = [jnp.exp(v - vals[0]) for v in vals]
    denom = exps[0]
    for e in exps[1:]:
        denom = denom + e
    r = lax.broadcasted_iota(jnp.int32, (tt, tt), 0)
    c = lax.broadcasted_iota(jnp.int32, (tt, tt), 1)
    lower = jnp.where(r > c, 1.0, 0.0).astype(BF16)
    before = jnp.dot(lower, onehot.astype(BF16), preferred_element_type=F32) + cnt[...]
    for k in range(top_k):
        idx_ref[:, k:k + 1] = idxs[k]
        w_ref[:, k:k + 1] = exps[k] / denom
        rk = jnp.sum(jnp.where(lane == idxs[k], before, 0.0), axis=-1, keepdims=True)
        rank_ref[:, k:k + 1] = rk.astype(jnp.int32)
    cnt[...] = cnt[...] + jnp.sum(onehot, axis=0, keepdims=True)
    cnt_ref[...] = cnt[...]


def _router(x, g, w_router, b_router, tt=256):
    t, d = x.shape
    n_exp = w_router.shape[1]
    tt = _tile(t, tt)
    kern = functools.partial(_router_kernel, n_exp=n_exp, top_k=TOP_K)
    tok_map = lambda i: (i, 0)
    const = lambda i: (0, 0)
    return pl.pallas_call(
        kern,
        grid=(t // tt,),
        in_specs=[pl.BlockSpec((tt, d), tok_map),
                  pl.BlockSpec((1, d), const),
                  pl.BlockSpec((d, n_exp), const),
                  pl.BlockSpec((1, n_exp), const)],
        out_specs=[pl.BlockSpec((tt, d // 2), tok_map),
                   pl.BlockSpec((tt, TOP_K), tok_map),
                   pl.BlockSpec((tt, TOP_K), tok_map),
                   pl.BlockSpec((tt, TOP_K), tok_map),
                   pl.BlockSpec((1, n_exp), const)],
        out_shape=[jax.ShapeDtypeStruct((t, d // 2), jnp.uint32),
                   jax.ShapeDtypeStruct((t, TOP_K), jnp.int32),
                   jax.ShapeDtypeStruct((t, TOP_K), F32),
                   jax.ShapeDtypeStruct((t, TOP_K), jnp.int32),
                   jax.ShapeDtypeStruct((1, n_exp), F32)],
        scratch_shapes=[pltpu.VMEM((1, n_exp), F32)],
        compiler_params=_params("arbitrary"),
        name="router",
    )(x, g.reshape(1, d), w_router, b_router.reshape(1, n_exp))


def _column_groups(width):
    return max(1, min(4, width // (2 * LANES)))


def _run_tables(block_e, n_used):
    n = block_e.shape[0]
    i = jnp.arange(n, dtype=jnp.int32)
    used = i < n_used[0]
    prev_e = jnp.concatenate([jnp.full((1,), -1, jnp.int32), block_e[:-1]])
    first = used & ((i == 0) | (block_e != prev_e))
    slot = (jnp.cumsum(first.astype(jnp.int32)) - 1) % 2
    jrun = i - lax.cummax(jnp.where(first, i, 0))
    first_pos = jnp.where(first, i, n)
    next_first = jnp.concatenate([lax.cummin(first_pos, reverse=True)[1:], jnp.full((1,), n, jnp.int32)])
    nxt = jnp.where(next_first < n, block_e[jnp.minimum(next_first, n - 1)], -1)
    plen = jnp.concatenate([jnp.zeros((1,), jnp.int32), jrun[:-1] + 1])
    as_i32 = lambda a: a.astype(jnp.int32)
    return as_i32(first), as_i32(slot), as_i32(jrun), as_i32(nxt), as_i32(plen)


def _weight_streamer(i, be_ref, first_ref, slot_ref, jrun_ref, nxt_ref, plen_ref,
                     w_hbm, wres, stage, wsem, rows):
    nslab = w_hbm.shape[1] // rows
    e = be_ref[i]
    slot = slot_ref[i]
    j = jrun_ref[i]
    nxt = nxt_ref[i]
    sem = wsem.at[0]

    def fetch(expert, c):
        return pltpu.make_async_copy(w_hbm.at[expert, pl.ds(c * rows, rows), :], stage, sem)

    def cast_into(dst, c):
        wres[dst, pl.ds(pl.multiple_of(c * rows, rows), rows), :] = stage[...].astype(BF16)

    def ensure():
        @pl.when(first_ref[i] == 1)
        def _():
            done = jnp.where(i == 0, 0, jnp.minimum(plen_ref[i], nslab))

            def body(c, carry):
                cp = fetch(e, c)
                cp.start()
                cp.wait()
                cast_into(slot, c)
                return carry

            lax.fori_loop(done, nslab, body, 0)

    prefetching = jnp.logical_and(j < nslab, nxt >= 0)

    def start():
        @pl.when(prefetching)
        def _():
            fetch(nxt, j).start()

    def finish():
        @pl.when(prefetching)
        def _():
            fetch(nxt, j).wait()
            cast_into(1 - slot, j)

    return slot, ensure, start, finish


def _expert_up_kernel(be_ref, nb_ref, tok_ref, first_ref, slot_ref, jrun_ref, nxt_ref, plen_ref,
                      h_hbm, w1_hbm, b1_ref, o_ref, xbuf, sem, wres, stage, wsem, *, tm, rows):
    i = pl.program_id(0)
    nb = nb_ref[0]

    def issue(blk, slot):
        base = blk * tm

        def body(r, carry):
            tok = tok_ref[base + r]
            pltpu.make_async_copy(h_hbm.at[pl.ds(tok, 1), :], xbuf.at[slot, pl.ds(r, 1), :],
                                  sem.at[slot]).start()
            return carry

        lax.fori_loop(0, tm, body, 0)

    def wait(slot):
        pltpu.make_async_copy(h_hbm.at[pl.ds(0, tm), :], xbuf.at[slot], sem.at[slot]).wait()

    @pl.when(i == 0)
    def _():
        issue(0, 0)
        issue(1, 1)

    @pl.when(i < nb)
    def _():
        wslot, ensure_weights, prefetch_start, prefetch_finish = _weight_streamer(
            i, be_ref, first_ref, slot_ref, jrun_ref, nxt_ref, plen_ref, w1_hbm, wres, stage, wsem, rows)
        ensure_weights()
        prefetch_start()
        cur = i % GATHER_SLOTS
        wait(cur)
        lo, hi = _unpack_pairs(xbuf[cur])
        ahead = (i + 2) % GATHER_SLOTS
        base = (i + 2) * tm
        half = lo.shape[1]
        lo_b = lo.astype(BF16)
        hi_b = hi.astype(BF16)
        f = o_ref.shape[1]
        lane = lax.broadcasted_iota(jnp.int32, (tm, LANES), 1)
        first = lane < LANES // 2
        idx_glu = (2 * lane) % LANES
        idx_lin = (2 * lane + 1) % LANES
        n_groups = _column_groups(f)
        gw = 2 * f // n_groups
        start_groups = max(1, n_groups - 1)
        for q in range(n_groups):
            share = range(q * tm // start_groups, (q + 1) * tm // start_groups) if q < start_groups else ()
            for r in share:
                pltpu.make_async_copy(h_hbm.at[pl.ds(tok_ref[base + r], 1), :],
                                      xbuf.at[ahead, pl.ds(r, 1), :], sem.at[ahead]).start()
            pin = lax.bitcast_convert_type(xbuf[cur, 0:8, 0:LANES], F32)
            cols = slice(q * gw, (q + 1) * gw)
            hh = (jnp.dot(lo_b, wres[wslot, :half, cols], preferred_element_type=F32)
                  + jnp.dot(hi_b, wres[wslot, half:, cols], preferred_element_type=F32)
                  + b1_ref[:, cols])
            acts = []
            for c in range(gw // (2 * LANES)):
                v0 = hh[:, 2 * c * LANES:(2 * c + 1) * LANES]
                v1 = hh[:, (2 * c + 1) * LANES:(2 * c + 2) * LANES]
                g = jnp.where(first, jnp.take_along_axis(v0, idx_glu, axis=1),
                              jnp.take_along_axis(v1, idx_glu, axis=1))
                li = jnp.where(first, jnp.take_along_axis(v0, idx_lin, axis=1),
                               jnp.take_along_axis(v1, idx_lin, axis=1))
                glu = jnp.minimum(g, SWIGLU_LIMIT)
                lin = jnp.clip(li, -SWIGLU_LIMIT, SWIGLU_LIMIT)
                act = glu * jax.nn.sigmoid(SWIGLU_ALPHA * glu) * (lin + 1.0)
                if c == 0:
                    never = glu[0:8] > SWIGLU_LIMIT
                    act = jnp.concatenate([jnp.where(never, pin, act[0:8]), act[8:]], axis=0)
                acts.append(act.astype(BF16))
            o_ref[:, q * (gw // 2):(q + 1) * (gw // 2)] = jnp.concatenate(acts, axis=1)
        prefetch_finish()

    @pl.when(i == nb)
    def _():
        wait(nb % GATHER_SLOTS)
        wait((nb + 1) % GATHER_SLOTS)

    @pl.when(i >= nb)
    def _():
        o_ref[...] = jnp.zeros_like(o_ref)


def _expert_down_kernel(be_ref, nb_ref, first_ref, slot_ref, jrun_ref, nxt_ref, plen_ref,
                        a_ref, w2_hbm, b2_ref, o_ref, wres, stage, wsem, *, rows):
    i = pl.program_id(0)
    nb = nb_ref[0]

    @pl.when(i < nb)
    def _():
        wslot, ensure_weights, prefetch_start, prefetch_finish = _weight_streamer(
            i, be_ref, first_ref, slot_ref, jrun_ref, nxt_ref, plen_ref, w2_hbm, wres, stage, wsem, rows)
        ensure_weights()
        prefetch_start()
        a = a_ref[...]
        half = o_ref.shape[1]
        gw = half // _column_groups(half)
        for q in range(half // gw):
            cl = slice(q * gw, (q + 1) * gw)
            ch = slice(half + q * gw, half + (q + 1) * gw)
            y_lo = jnp.dot(a, wres[wslot, :, cl], preferred_element_type=F32) + b2_ref[:, cl]
            y_hi = jnp.dot(a, wres[wslot, :, ch], preferred_element_type=F32) + b2_ref[:, ch]
            o_ref[:, cl] = _pack_pairs(jnp.concatenate([y_lo, y_hi], axis=1))
        prefetch_finish()

    @pl.when(i >= nb)
    def _():
        o_ref[...] = jnp.zeros_like(o_ref)


def _experts(hp, block_e, n_used, row_tok, w1, b1, w2, b2, tm):
    t, half = hp.shape
    n_exp, f, d = w2.shape
    assert f % (2 * LANES) == 0 and d == 2 * half
    n_blocks = row_tok.shape[0] // tm - 1
    n_rows = n_blocks * tm
    assert (t * TOP_K) % tm == 0 and n_blocks == (t * TOP_K) // tm + n_exp
    tables = _run_tables(block_e, n_used)
    rows1 = _tile(d, W1_SLAB_ROWS)
    rows2 = _tile(f, W2_SLAB_ROWS)

    def emap(i, be, *_):
        return (be[i], 0, 0)

    def rmap(i, *_):
        return (i, 0)

    up_spec = pltpu.PrefetchScalarGridSpec(
        num_scalar_prefetch=8,
        grid=(n_blocks,),
        in_specs=[pl.BlockSpec(memory_space=pl.ANY),
                  pl.BlockSpec(memory_space=pl.ANY),
                  pl.BlockSpec((None, 1, 2 * f), emap)],
        out_specs=pl.BlockSpec((tm, f), rmap),
        scratch_shapes=[pltpu.VMEM((GATHER_SLOTS, tm, half), jnp.uint32),
                        pltpu.SemaphoreType.DMA((GATHER_SLOTS,)),
                        pltpu.VMEM((2, d, 2 * f), BF16),
                        pltpu.VMEM((rows1, 2 * f), F32),
                        pltpu.SemaphoreType.DMA((1,))],
    )
    act = pl.pallas_call(
        functools.partial(_expert_up_kernel, tm=tm, rows=rows1),
        grid_spec=up_spec,
        out_shape=jax.ShapeDtypeStruct((n_rows, f), BF16),
        compiler_params=_params("arbitrary"),
        name="experts_up",
    )(block_e, n_used, row_tok, *tables, hp, w1, b1)

    down_spec = pltpu.PrefetchScalarGridSpec(
        num_scalar_prefetch=7,
        grid=(n_blocks,),
        in_specs=[pl.BlockSpec((tm, f), rmap),
                  pl.BlockSpec(memory_space=pl.ANY),
                  pl.BlockSpec((None, 1, d), emap)],
        out_specs=pl.BlockSpec((tm, half), rmap),
        scratch_shapes=[pltpu.VMEM((2, f, d), BF16),
                        pltpu.VMEM((rows2, d), F32),
                        pltpu.SemaphoreType.DMA((1,))],
    )
    return pl.pallas_call(
        functools.partial(_expert_down_kernel, rows=rows2),
        grid_spec=down_spec,
        out_shape=jax.ShapeDtypeStruct((n_rows, half), jnp.uint32),
        compiler_params=_params("arbitrary"),
        name="experts_down",
    )(block_e, n_used, *tables, act, w2, b2)


def _combine_kernel(pos_ref, x_ref, w_ref, g_ref, y_hbm, o_ref, ybuf, sem, *, tc, top_k, normalize):
    i = pl.program_id(0)
    n = pl.num_programs(0)

    def issue(blk, slot):
        base = blk * (tc * top_k)

        def body(r, carry):
            for k in range(top_k):
                p = pos_ref[base + r * top_k + k]
                pltpu.make_async_copy(y_hbm.at[pl.ds(p, 1), :], ybuf.at[slot, k, pl.ds(r, 1), :],
                                      sem.at[slot]).start()
            return carry

        lax.fori_loop(0, tc, body, 0)

    def wait(slot):
        for k in range(top_k):
            pltpu.make_async_copy(y_hbm.at[pl.ds(0, tc), :], ybuf.at[slot, k], sem.at[slot]).wait()

    @pl.when(i == 0)
    def _():
        issue(0, 0)
        issue(1, 1)

    slot = i % GATHER_SLOTS
    wait(slot)
    w = w_ref[...]
    x = x_ref[...]
    half = x.shape[1] // 2
    acc_lo = x[:, :half]
    acc_hi = x[:, half:]
    for k in range(top_k):
        lo, hi = _unpack_pairs(ybuf[slot, k])
        acc_lo = acc_lo + lo * w[:, k:k + 1]
        acc_hi = acc_hi + hi * w[:, k:k + 1]
    acc = jnp.concatenate([acc_lo, acc_hi], axis=1)
    if normalize:
        ms = jnp.mean(acc * acc, axis=-1, keepdims=True)
        acc = acc * lax.rsqrt(ms + NORM_EPS) * g_ref[...]
    o_ref[...] = acc
    ahead = (i + 2) % GATHER_SLOTS
    base = (i + 2) * (tc * top_k)
    for r in range(tc):
        for k in range(top_k):
            pltpu.make_async_copy(y_hbm.at[pl.ds(pos_ref[base + r * top_k + k], 1), :],
                                  ybuf.at[ahead, k, pl.ds(r, 1), :], sem.at[ahead]).start()

    @pl.when(i == n - 1)
    def _():
        wait((n % GATHER_SLOTS))
        wait(((n + 1) % GATHER_SLOTS))


def _combine(x, top_w, pos, y_rows, g, normalize, tc=128):
    t, d = x.shape
    tc = _tile(t, tc)
    assert t // tc >= 2
    pos = jnp.concatenate([pos.reshape(-1), jnp.zeros((2 * tc * TOP_K,), jnp.int32)])
    grid_spec = pltpu.PrefetchScalarGridSpec(
        num_scalar_prefetch=1,
        grid=(t // tc,),
        in_specs=[pl.BlockSpec((tc, d), lambda i, p: (i, 0)),
                  pl.BlockSpec((tc, TOP_K), lambda i, p: (i, 0)),
                  pl.BlockSpec((1, d), lambda i, p: (0, 0)),
                  pl.BlockSpec(memory_space=pl.ANY)],
        out_specs=pl.BlockSpec((tc, d), lambda i, p: (i, 0)),
        scratch_shapes=[pltpu.VMEM((GATHER_SLOTS, TOP_K, tc, d // 2), jnp.uint32),
                        pltpu.SemaphoreType.DMA((GATHER_SLOTS,))],
    )
    return pl.pallas_call(
        functools.partial(_combine_kernel, tc=tc, top_k=TOP_K, normalize=normalize),
        grid_spec=grid_spec,
        out_shape=jax.ShapeDtypeStruct((t, d), F32),
        compiler_params=_params("arbitrary"),
        name="combine",
    )(pos, x, top_w, g.reshape(1, d), y_rows)


def _moe_block_rows(n_assign):
    return 256 if n_assign >= 256 * 32 else 128


def kernel(x, mem, positions, norm_mix_g, w_in, w_s, b_s, ln_v_g, ln_v_b, beta_a, beta_b, w_out,
           norm_x_g, norm_mem_g, w_xq, w_xkv, w_xo, norm_ffn_g, w_router, b_router, w1, b1, w2, b2,
           norm_final_g):
    bsz, s, d = x.shape
    t = bsz * s
    depth = w_in.shape[0]
    chunk = w_s.shape[2]
    n_exp = w_router.shape[2]
    xs = x.reshape(t, d)
    mem2 = mem.reshape(-1, d)

    for l in range(depth):
        h = _rmsnorm(xs, norm_mix_g[l], BF16)
        proj = _matmul([h], w_in[l], BF16, name="w_in")
        out_a = _mixer_a(proj, w_s[l], b_s[l], ln_v_g[l], ln_v_b[l], beta_a[l])
        out_b = _mixer_b(proj, positions, beta_b[l], chunk, B_HEADS)
        xs = _matmul([out_a, out_b], w_out[l], F32, res=xs, name="w_out")

        h = _rmsnorm(xs, norm_x_g[l], BF16)
        mem_n = _rmsnorm(mem2, norm_mem_g[l], BF16)
        q = _matmul([h], w_xq[l], BF16, name="w_xq")
        kv = _matmul([mem_n], w_xkv[l], BF16, name="w_xkv")
        o = _xattn(q, kv, bsz, X_HEADS)
        xs = _matmul([o], w_xo[l], F32, res=xs, name="w_xo")

        hf, top_idx, top_w, rank, counts = _router(xs, norm_ffn_g[l], w_router[l], b_router[l])
        n_assign = t * TOP_K
        tm = _moe_block_rows(n_assign)
        counts = counts.reshape(n_exp).astype(jnp.int32)
        padded = (counts + tm - 1) // tm * tm
        pends = jnp.cumsum(padded)
        pstarts = pends - padded
        pos = pstarts[top_idx] + rank
        n_blocks = -(-n_assign // tm) + n_exp
        n_rows = n_blocks * tm
        row_tok = jnp.zeros((n_rows + tm,), jnp.int32).at[pos.reshape(-1)].set(
            jnp.arange(n_assign, dtype=jnp.int32) // TOP_K)
        block_start = jnp.arange(n_blocks, dtype=jnp.int32) * tm
        block_e = jnp.sum((pends[None, :] <= block_start[:, None]).astype(jnp.int32), axis=1)
        block_e = jnp.minimum(block_e, n_exp - 1)
        n_used = (pends[-1:] // tm).astype(jnp.int32)
        f = w2.shape[2]
        y_rows = _experts(hf, block_e, n_used, row_tok, w1[l], b1[l].reshape(n_exp, 1, 2 * f),
                          w2[l], b2[l].reshape(n_exp, 1, d), tm)
        xs = _combine(xs, top_w, pos, y_rows, norm_final_g, normalize=(l + 1 == depth))
    return xs.reshape(bsz, s, d)
```

```python
import functools
import math

import jax
import jax.numpy as jnp
from jax import lax
from jax.experimental import pallas as pl
from jax.experimental.pallas import tpu as pltpu

B_HEADS = 8
X_HEADS = 4
TOP_K = 4
ROPE_BASE = 10000.0
NORM_EPS = 1e-5
RET_NORM_EPS = 1e-6
SWIGLU_LIMIT = 7.0
SWIGLU_ALPHA = 1.702

LANES = 128
GATHER_SLOTS = 3
W1_SLAB_ROWS = 512
W2_SLAB_ROWS = 128
SLAB_COPIES = 4

V7X_VMEM_BYTES = 64 * 1024 * 1024
VMEM_LIMIT_BYTES = V7X_VMEM_BYTES - 6 * 1024 * 1024

F32 = jnp.float32
BF16 = jnp.bfloat16


def _params(*semantics):
    return pltpu.CompilerParams(
        dimension_semantics=semantics,
        vmem_limit_bytes=VMEM_LIMIT_BYTES,
        disable_bounds_checks=True,
    )


def _tile(n, want):
    t = min(n, want)
    while n % t:
        t //= 2
    return t


def _pack_pairs(v):
    half = v.shape[1] // 2
    bits = lax.bitcast_convert_type(v.astype(BF16).astype(F32), jnp.uint32)
    return (bits[:, :half] >> 16) | (bits[:, half:] & jnp.uint32(0xFFFF0000))


def _store_row_slabs(ref, first_chunk, words):
    m, w = words.shape
    r = ref.shape[0] // m
    for s in range(w // LANES):
        ref[pl.ds(first_chunk + s, m, stride=r), :] = words[:, s * LANES:(s + 1) * LANES]


def _load_row_slabs(read_rows, m, r):
    return jnp.concatenate([read_rows(pl.ds(s, m, stride=r)) for s in range(r)], axis=1)


def _unpack_pairs(p):
    lo = lax.bitcast_convert_type(p << 16, F32)
    hi = lax.bitcast_convert_type(p & jnp.uint32(0xFFFF0000), F32)
    return lo, hi


def _rms_kernel(x_ref, g_ref, o_ref):
    x = x_ref[...]
    ms = jnp.mean(x * x, axis=-1, keepdims=True)
    o_ref[...] = (x * lax.rsqrt(ms + NORM_EPS) * g_ref[...]).astype(o_ref.dtype)


def _rmsnorm(x, g, out_dtype, tm=512):
    m, d = x.shape
    tm = _tile(m, tm)
    return pl.pallas_call(
        _rms_kernel,
        grid=(m // tm,),
        in_specs=[pl.BlockSpec((tm, d), lambda i: (i, 0)),
                  pl.BlockSpec((1, d), lambda i: (0, 0))],
        out_specs=pl.BlockSpec((tm, d), lambda i: (i, 0)),
        out_shape=jax.ShapeDtypeStruct((m, d), out_dtype),
        compiler_params=_params("arbitrary"),
        name="rmsnorm",
    )(x, g.reshape(1, d))


def _mm_kernel(*refs, n_a, has_res):
    a_refs = refs[:n_a]
    w_ref = refs[n_a]
    r_ref = refs[n_a + 1] if has_res else None
    o_ref, wbf = refs[-2:]

    @pl.when(pl.program_id(1) == 0)
    def _():
        wbf[...] = w_ref[...].astype(BF16)

    acc = None
    k0 = 0
    for a_ref in a_refs:
        ka = a_ref.shape[1]
        part = jnp.dot(a_ref[...], wbf[k0:k0 + ka, :], preferred_element_type=F32)
        acc = part if acc is None else acc + part
        k0 += ka
    if has_res:
        acc = acc + r_ref[...]
    o_ref[...] = acc.astype(o_ref.dtype)


def _matmul(a_parts, w, out_dtype, res=None, tm=1024, tn=512, name="matmul"):
    m = a_parts[0].shape[0]
    k, n = w.shape
    assert sum(a.shape[1] for a in a_parts) == k
    tm = _tile(m, tm)
    tn = _tile(n, tn)
    in_specs = [pl.BlockSpec((tm, a.shape[1]), lambda j, i: (i, 0)) for a in a_parts]
    in_specs.append(pl.BlockSpec((k, tn), lambda j, i: (0, j)))
    args = list(a_parts) + [w]
    if res is not None:
        in_specs.append(pl.BlockSpec((tm, tn), lambda j, i: (i, j)))
        args.append(res)
    return pl.pallas_call(
        functools.partial(_mm_kernel, n_a=len(a_parts), has_res=res is not None),
        grid=(n // tn, m // tm),
        in_specs=in_specs,
        out_specs=pl.BlockSpec((tm, tn), lambda j, i: (i, j)),
        out_shape=jax.ShapeDtypeStruct((m, n), out_dtype),
        scratch_shapes=[pltpu.VMEM((k, tn), BF16)],
        compiler_params=_params("arbitrary", "arbitrary"),
        name=name,
    )(*args)


def _amix_kernel(a_ref, ws_ref, bst_ref, lng_ref, lnb_ref, beta_ref, o_ref, *, chunk, heads, hd):
    aw = heads * hd
    a = a_ref[...].astype(F32)
    hid = 0.5 * a * (1.0 + lax.erf(a * (2.0 ** -0.5)))
    u = hid[:, :aw]
    v = hid[:, aw:]
    mean = jnp.mean(v, axis=-1, keepdims=True)
    vc = v - mean
    var = jnp.mean(vc * vc, axis=-1, keepdims=True)
    vn = (vc * lax.rsqrt(var + NORM_EPS) * lng_ref[...] + lnb_ref[...]).astype(BF16)
    ub = u * beta_ref[...]
    rows = a.shape[0]
    row = lax.broadcasted_iota(jnp.int32, (chunk, chunk), 0)
    col = lax.broadcasted_iota(jnp.int32, (chunk, chunk), 1)
    causal = row >= col
    bst = bst_ref[...]
    for h in range(heads):
        w = jnp.where(causal, ws_ref[h], 0.0).astype(BF16)
        bias = bst[:, h:h + 1]
        for c in range(rows // chunk):
            rs = slice(c * chunk, (c + 1) * chunk)
            cs = slice(h * hd, (h + 1) * hd)
            mixed = jnp.dot(w, vn[rs, cs], preferred_element_type=F32) + bias
            o_ref[rs, cs] = (ub[rs, cs] * mixed).astype(o_ref.dtype)


def _mixer_a(proj, w_s, b_s, ln_g, ln_b, beta, rows=256):
    t = proj.shape[0]
    heads, chunk, _ = w_s.shape
    aw = ln_g.shape[0]
    hd = aw // heads
    rows = _tile(t, rows)
    assert rows % chunk == 0
    kern = functools.partial(_amix_kernel, chunk=chunk, heads=heads, hd=hd)
    return pl.pallas_call(
        kern,
        grid=(t // rows,),
        in_specs=[pl.BlockSpec((rows, 2 * aw), lambda i: (i, 0)),
                  pl.BlockSpec((heads, chunk, chunk), lambda i: (0, 0, 0)),
                  pl.BlockSpec((chunk, heads), lambda i: (0, 0)),
                  pl.BlockSpec((1, aw), lambda i: (0, 0)),
                  pl.BlockSpec((1, aw), lambda i: (0, 0)),
                  pl.BlockSpec((1, aw), lambda i: (0, 0))],
        out_specs=pl.BlockSpec((rows, aw), lambda i: (i, 0)),
        out_shape=jax.ShapeDtypeStruct((t, aw), BF16),
        compiler_params=_params("arbitrary"),
        name="mixer_a",
    )(proj, w_s, b_s.T, ln_g.reshape(1, aw), ln_b.reshape(1, aw), beta.reshape(1, aw))


def _ret_kernel(pos_ref, invf_ref, din_ref, qd_ref, kd_ref, q_ref, k_ref, v_ref, g_ref, beta_ref,
                o_ref, state, *, heads, hd, chunk):
    @pl.when(pl.program_id(1) == 0)
    def _():
        state[...] = jnp.zeros_like(state)

    half = hd // 2
    ang = pos_ref[...] * invf_ref[...]
    cos = jnp.cos(ang)
    sin = jnp.sin(ang)
    qd_all = qd_ref[...]
    kd_all = kd_ref[...]
    k_scale = hd ** -0.5

    def rot(t):
        t1 = t[:, :half]
        t2 = t[:, half:]
        return jnp.concatenate([t1 * cos - t2 * sin, t2 * cos + t1 * sin], axis=-1)

    for h in range(heads):
        cs = slice(h * hd, (h + 1) * hd)
        chunk_decay = math.exp(math.log(1.0 - 2.0 ** (-5.0 - h)) * chunk)
        qr = rot(q_ref[:, cs].astype(F32))
        kr = rot(k_ref[:, cs].astype(F32)) * k_scale
        vb = v_ref[:, cs]
        scores = lax.dot_general(qr.astype(BF16), kr.astype(BF16), (((1,), (1,)), ((), ())),
                                 preferred_element_type=F32) * din_ref[h]
        inner = jnp.dot(scores.astype(BF16), vb, preferred_element_type=F32)
        st = state[h]
        cross = jnp.dot((qr * qd_all[:, h:h + 1]).astype(BF16), st.astype(BF16),
                        preferred_element_type=F32)
        kv = lax.dot_general((kr * kd_all[:, h:h + 1]).astype(BF16), vb, (((0,), (0,)), ((), ())),
                             preferred_element_type=F32)
        state[h] = st * chunk_decay + kv
        out = inner + cross
        out = out * lax.rsqrt(jnp.mean(out * out, axis=-1, keepdims=True) + RET_NORM_EPS)
        g = g_ref[:, cs].astype(F32)
        out = out * (g * jax.nn.sigmoid(g)) * beta_ref[:, cs]
        o_ref[:, cs] = out.astype(o_ref.dtype)


def _mixer_b(proj, positions, beta, chunk, heads):
    t = proj.shape[0]
    bsz, s = positions.shape
    bw = beta.shape[0]
    hd = bw // heads
    half = hd // 2
    n_chunks = s // chunk
    assert proj.shape[1] == 6 * bw
    log_gamma = jnp.log(1.0 - 2.0 ** (-5.0 - jnp.arange(heads, dtype=F32)))
    idx = jnp.arange(chunk, dtype=F32)
    diff = idx[:, None] - idx[None, :]
    decay_in = jnp.where(diff >= 0, jnp.exp(log_gamma[:, None, None] * jnp.maximum(diff, 0.0)), 0.0)
    q_decay = jnp.exp(log_gamma[None, :] * (idx[:, None] + 1.0))
    k_decay = jnp.exp(log_gamma[None, :] * (chunk - 1.0 - idx[:, None]))
    inv_freq = (ROPE_BASE ** (-jnp.arange(half, dtype=F32) / half)).reshape(1, half)
    pos = positions.astype(F32).reshape(t, 1)

    kern = functools.partial(_ret_kernel, heads=heads, hd=hd, chunk=chunk)
    row_map = lambda b, c: (b * n_chunks + c, 0)
    const2 = lambda b, c: (0, 0)

    def col_block(jb):
        return pl.BlockSpec((chunk, bw), lambda b, c: (b * n_chunks + c, jb))

    return pl.pallas_call(
        kern,
        grid=(bsz, n_chunks),
        in_specs=[pl.BlockSpec((chunk, 1), row_map),
                  pl.BlockSpec((1, half), const2),
                  pl.BlockSpec((heads, chunk, chunk), lambda b, c: (0, 0, 0)),
                  pl.BlockSpec((chunk, heads), const2),
                  pl.BlockSpec((chunk, heads), const2),
                  col_block(2), col_block(3), col_block(4), col_block(5),
                  pl.BlockSpec((1, bw), const2)],
        out_specs=pl.BlockSpec((chunk, bw), row_map),
        out_shape=jax.ShapeDtypeStruct((t, bw), BF16),
        scratch_shapes=[pltpu.VMEM((heads, hd, hd), F32)],
        compiler_params=_params("arbitrary", "arbitrary"),
        name="mixer_b",
    )(pos, inv_freq, decay_in, q_decay, k_decay, proj, proj, proj, proj, beta.reshape(1, bw))


def _xattn_kernel(q_ref, k_ref, v_ref, o_ref, *, scale):
    s = lax.dot_general(q_ref[...], k_ref[...], (((1,), (1,)), ((), ())),
                        preferred_element_type=F32) * scale
    m = jnp.max(s, axis=-1, keepdims=True)
    p = jnp.exp(s - m)
    l = jnp.sum(p, axis=-1, keepdims=True)
    o = jnp.dot(p.astype(BF16), v_ref[...], preferred_element_type=F32)
    o_ref[...] = (o / l).astype(o_ref.dtype)


def _xattn(q, kv, bsz, heads, tq=512):
    t, d = q.shape
    s = t // bsz
    m = kv.shape[0] // bsz
    dh = d // heads
    tq = _tile(s, tq)
    nq = s // tq
    return pl.pallas_call(
        functools.partial(_xattn_kernel, scale=dh ** -0.5),
        grid=(bsz, nq, heads),
        in_specs=[pl.BlockSpec((tq, dh), lambda b, i, h: (b * nq + i, h)),
                  pl.BlockSpec((m, dh), lambda b, i, h: (b, h)),
                  pl.BlockSpec((m, dh), lambda b, i, h: (b, heads + h))],
        out_specs=pl.BlockSpec((tq, dh), lambda b, i, h: (b * nq + i, h)),
        out_shape=jax.ShapeDtypeStruct((t, d), BF16),
        compiler_params=_params("arbitrary", "arbitrary", "arbitrary"),
        name="xattn",
    )(q, kv, kv)


def _router_kernel(x_ref, g_ref, wr_ref, br_ref, h_ref, idx_ref, w_ref, rank_ref, cnt_ref, cnt, *,
                   n_exp, top_k):
    @pl.when(pl.program_id(0) == 0)
    def _():
        cnt[...] = jnp.zeros_like(cnt)

    x = x_ref[...]
    ms = jnp.mean(x * x, axis=-1, keepdims=True)
    h = x * lax.rsqrt(ms + NORM_EPS) * g_ref[...]
    _store_row_slabs(h_ref, 0, _pack_pairs(h))
    wr = wr_ref[...]
    h_hi = h.astype(BF16)
    h_lo = (h - h_hi.astype(F32)).astype(BF16)
    w_hi = wr.astype(BF16)
    w_lo = (wr - w_hi.astype(F32)).astype(BF16)
    logits = (jnp.dot(h_hi, w_hi, preferred_element_type=F32)
              + jnp.dot(h_hi, w_lo, preferred_element_type=F32)
              + jnp.dot(h_lo, w_hi, preferred_element_type=F32)) + br_ref[...]
    tt = x.shape[0]
    lane = lax.broadcasted_iota(jnp.int32, (tt, n_exp), 1)
    work = logits
    vals, idxs = [], []
    onehot = jnp.zeros((tt, n_exp), F32)
    for _ in range(top_k):
        mx = jnp.max(work, axis=-1, keepdims=True)
        ix = jnp.min(jnp.where(work == mx, lane, n_exp), axis=-1, keepdims=True)
        sel = lane == ix
        onehot = jnp.where(sel, 1.0, onehot)
        work = jnp.where(sel, -jnp.inf, work)
        vals.append(mx)
        idxs.append(ix)
    exps = [jnp.exp(v - vals[0]) for v in vals]
    denom = exps[0]
    for e in exps[1:]:
        denom = denom + e
    r = lax.broadcasted_iota(jnp.int32, (tt, tt), 0)
    c = lax.broadcasted_iota(jnp.int32, (tt, tt), 1)
    lower = jnp.where(r > c, 1.0, 0.0).astype(BF16)
    before = jnp.dot(lower, onehot.astype(BF16), preferred_element_type=F32) + cnt[...]
    for k in range(top_k):
        idx_ref[:, k:k + 1] = idxs[k]
        w_ref[:, k:k + 1] = exps[k] / denom
        rk = jnp.sum(jnp.where(lane == idxs[k], before, 0.0), axis=-1, keepdims=True)
        rank_ref[:, k:k + 1] = rk.astype(jnp.int32)
    cnt[...] = cnt[...] + jnp.sum(onehot, axis=0, keepdims=True)
    cnt_ref[...] = cnt[...]


def _router(x, g, w_router, b_router, tt=256):
    t, d = x.shape
    n_exp = w_router.shape[1]
    tt = _tile(t, tt)
    kern = functools.partial(_router_kernel, n_exp=n_exp, top_k=TOP_K)
    tok_map = lambda i: (i, 0)
    const = lambda i: (0, 0)
    return pl.pallas_call(
        kern,
        grid=(t // tt,),
        in_specs=[pl.BlockSpec((tt, d), tok_map),
                  pl.BlockSpec((1, d), const),
                  pl.BlockSpec((d, n_exp), const),
                  pl.BlockSpec((1, n_exp), const)],
        out_specs=[pl.BlockSpec((tt * (d // 2 // LANES), LANES), tok_map),
                   pl.BlockSpec((tt, TOP_K), tok_map),
                   pl.BlockSpec((tt, TOP_K), tok_map),
                   pl.BlockSpec((tt, TOP_K), tok_map),
                   pl.BlockSpec((1, n_exp), const)],
        out_shape=[jax.ShapeDtypeStruct((t * (d // 2 // LANES), LANES), jnp.uint32),
                   jax.ShapeDtypeStruct((t, TOP_K), jnp.int32),
                   jax.ShapeDtypeStruct((t, TOP_K), F32),
                   jax.ShapeDtypeStruct((t, TOP_K), jnp.int32),
                   jax.ShapeDtypeStruct((1, n_exp), F32)],
        scratch_shapes=[pltpu.VMEM((1, n_exp), F32)],
        compiler_params=_params("arbitrary"),
        name="router",
    )(x, g.reshape(1, d), w_router, b_router.reshape(1, n_exp))


def _column_groups(width):
    return max(1, min(4, width // (2 * LANES)))


def _run_tables(block_e, n_used):
    n = block_e.shape[0]
    i = jnp.arange(n, dtype=jnp.int32)
    used = i < n_used[0]
    prev_e = jnp.concatenate([jnp.full((1,), -1, jnp.int32), block_e[:-1]])
    first = used & ((i == 0) | (block_e != prev_e))
    slot = (jnp.cumsum(first.astype(jnp.int32)) - 1) % 2
    jrun = i - lax.cummax(jnp.where(first, i, 0))
    first_pos = jnp.where(first, i, n)
    next_first = jnp.concatenate([lax.cummin(first_pos, reverse=True)[1:], jnp.full((1,), n, jnp.int32)])
    nxt = jnp.where(next_first < n, block_e[jnp.minimum(next_first, n - 1)], -1)
    plen = jnp.concatenate([jnp.zeros((1,), jnp.int32), jrun[:-1] + 1])
    as_i32 = lambda a: a.astype(jnp.int32)
    return as_i32(first), as_i32(slot), as_i32(jrun), as_i32(nxt), as_i32(plen)


def _weight_streamer(i, be_ref, first_ref, slot_ref, jrun_ref, nxt_ref, plen_ref,
                     w_hbm, wres, stage, wsem, rows):
    nslab = w_hbm.shape[1] // rows
    e = be_ref[i]
    slot = slot_ref[i]
    j = jrun_ref[i]
    nxt = nxt_ref[i]
    sem = wsem.at[0]

    def fetch(expert, c):
        sub = rows // SLAB_COPIES
        return [pltpu.make_async_copy(w_hbm.at[expert, pl.ds(c * rows + p * sub, sub), :],
                                      stage.at[pl.ds(p * sub, sub), :], sem)
                for p in range(SLAB_COPIES)]

    def cast_into(dst, c):
        wres[dst, pl.ds(pl.multiple_of(c * rows, rows), rows), :] = stage[...].astype(BF16)

    def ensure():
        @pl.when(first_ref[i] == 1)
        def _():
            done = jnp.where(i == 0, 0, jnp.minimum(plen_ref[i], nslab))

            def body(c, carry):
                copies = fetch(e, c)
                for cp in copies:
                    cp.start()
                for cp in copies:
                    cp.wait()
                cast_into(slot, c)
                return carry

            lax.fori_loop(done, nslab, body, 0)

    prefetching = jnp.logical_and(j < nslab, nxt >= 0)

    def start():
        @pl.when(prefetching)
        def _():
            for cp in fetch(nxt, j):
                cp.start()

    def finish():
        @pl.when(prefetching)
        def _():
            for cp in fetch(nxt, j):
                cp.wait()
            cast_into(1 - slot, j)

    return slot, ensure, start, finish


def _expert_up_kernel(be_ref, nb_ref, tok_ref, first_ref, slot_ref, jrun_ref, nxt_ref, plen_ref,
                      h_hbm, w1_hbm, b1_ref, o_ref, xbuf, sem, wres, stage, wsem, *, tm, rows):
    i = pl.program_id(0)
    nb = nb_ref[0]

    rpt = xbuf.shape[1] // tm

    def row_copy(tok, r, slot):
        return pltpu.make_async_copy(h_hbm.at[pl.ds(tok * rpt, rpt), :],
                                     xbuf.at[slot, pl.ds(r * rpt, rpt), :], sem.at[slot])

    def issue(blk, slot):
        base = blk * tm

        def body(r, carry):
            row_copy(tok_ref[base + r], r, slot).start()
            return carry

        lax.fori_loop(0, tm, body, 0)

    def wait(slot):
        pltpu.make_async_copy(h_hbm.at[pl.ds(0, tm * rpt), :], xbuf.at[slot], sem.at[slot]).wait()

    @pl.when(i == 0)
    def _():
        issue(0, 0)
        issue(1, 1)

    @pl.when(i < nb)
    def _():
        wslot, ensure_weights, prefetch_start, prefetch_finish = _weight_streamer(
            i, be_ref, first_ref, slot_ref, jrun_ref, nxt_ref, plen_ref, w1_hbm, wres, stage, wsem, rows)
        ensure_weights()
        prefetch_start()
        cur = i % GATHER_SLOTS
        wait(cur)
        lo, hi = _unpack_pairs(_load_row_slabs(lambda rows_: xbuf[cur, rows_, :], tm, rpt))
        ahead = (i + 2) % GATHER_SLOTS
        base = (i + 2) * tm
        half = lo.shape[1]
        lo_b = lo.astype(BF16)
        hi_b = hi.astype(BF16)
        f = o_ref.shape[1]
        lane = lax.broadcasted_iota(jnp.int32, (tm, LANES), 1)
        first = lane < LANES // 2
        idx_glu = (2 * lane) % LANES
        idx_lin = (2 * lane + 1) % LANES
        n_groups = _column_groups(f)
        gw = 2 * f // n_groups
        start_groups = max(1, n_groups - 1)
        for q in range(n_groups):
            share = range(q * tm // start_groups, (q + 1) * tm // start_groups) if q < start_groups else ()
            for r in share:
                row_copy(tok_ref[base + r], r, ahead).start()
            pin = lax.bitcast_convert_type(xbuf[cur, 0:8, 0:LANES], F32)
            cols = slice(q * gw, (q + 1) * gw)
            hh = (jnp.dot(lo_b, wres[wslot, :half, cols], preferred_element_type=F32)
                  + jnp.dot(hi_b, wres[wslot, half:, cols], preferred_element_type=F32)
                  + b1_ref[:, cols])
            acts = []
            for c in range(gw // (2 * LANES)):
                v0 = hh[:, 2 * c * LANES:(2 * c + 1) * LANES]
                v1 = hh[:, (2 * c + 1) * LANES:(2 * c + 2) * LANES]
                g = jnp.where(first, jnp.take_along_axis(v0, idx_glu, axis=1),
                              jnp.take_along_axis(v1, idx_glu, axis=1))
                li = jnp.where(first, jnp.take_along_axis(v0, idx_lin, axis=1),
                               jnp.take_along_axis(v1, idx_lin, axis=1))
                glu = jnp.minimum(g, SWIGLU_LIMIT)
                lin = jnp.clip(li, -SWIGLU_LIMIT, SWIGLU_LIMIT)
                act = glu * jax.nn.sigmoid(SWIGLU_ALPHA * glu) * (lin + 1.0)
                if c == 0:
                    never = glu[0:8] > SWIGLU_LIMIT
                    act = jnp.concatenate([jnp.where(never, pin, act[0:8]), act[8:]], axis=0)
                acts.append(act.astype(BF16))
            o_ref[:, q * (gw // 2):(q + 1) * (gw // 2)] = jnp.concatenate(acts, axis=1)
        prefetch_finish()

    @pl.when(i == nb)
    def _():
        wait(nb % GATHER_SLOTS)
        wait((nb + 1) % GATHER_SLOTS)

    @pl.when(i >= nb)
    def _():
        o_ref[...] = jnp.zeros_like(o_ref)


def _expert_down_kernel(be_ref, nb_ref, first_ref, slot_ref, jrun_ref, nxt_ref, plen_ref,
                        a_ref, w2_hbm, b2_ref, o_ref, wres, stage, wsem, *, rows):
    i = pl.program_id(0)
    nb = nb_ref[0]

    @pl.when(i < nb)
    def _():
        wslot, ensure_weights, prefetch_start, prefetch_finish = _weight_streamer(
            i, be_ref, first_ref, slot_ref, jrun_ref, nxt_ref, plen_ref, w2_hbm, wres, stage, wsem, rows)
        ensure_weights()
        prefetch_start()
        a = a_ref[...]
        half = wres.shape[2] // 2
        gw = half // _column_groups(half)
        for q in range(half // gw):
            cl = slice(q * gw, (q + 1) * gw)
            ch = slice(half + q * gw, half + (q + 1) * gw)
            y_lo = jnp.dot(a, wres[wslot, :, cl], preferred_element_type=F32) + b2_ref[:, cl]
            y_hi = jnp.dot(a, wres[wslot, :, ch], preferred_element_type=F32) + b2_ref[:, ch]
            o_ref[:, cl] = _pack_pairs(jnp.concatenate([y_lo, y_hi], axis=1))
        prefetch_finish()

    @pl.when(i >= nb)
    def _():
        o_ref[...] = jnp.zeros_like(o_ref)


def _experts(hp, block_e, n_used, row_tok, w1, b1, w2, b2, tm):
    n_exp, f, d = w2.shape
    half = d // 2
    rpt = half // LANES
    t = hp.shape[0] // rpt
    assert f % (2 * LANES) == 0 and hp.shape == (t * rpt, LANES)
    n_blocks = row_tok.shape[0] // tm - 1
    n_rows = n_blocks * tm
    assert (t * TOP_K) % tm == 0 and n_blocks == (t * TOP_K) // tm + n_exp
    tables = _run_tables(block_e, n_used)
    rows1 = _tile(d, W1_SLAB_ROWS)
    rows2 = _tile(f, W2_SLAB_ROWS)

    def emap(i, be, *_):
        return (be[i], 0, 0)

    def rmap(i, *_):
        return (i, 0)

    up_spec = pltpu.PrefetchScalarGridSpec(
        num_scalar_prefetch=8,
        grid=(n_blocks,),
        in_specs=[pl.BlockSpec(memory_space=pl.ANY),
                  pl.BlockSpec(memory_space=pl.ANY),
                  pl.BlockSpec((None, 1, 2 * f), emap)],
        out_specs=pl.BlockSpec((tm, f), rmap),
        scratch_shapes=[pltpu.VMEM((GATHER_SLOTS, tm * rpt, LANES), jnp.uint32),
                        pltpu.SemaphoreType.DMA((GATHER_SLOTS,)),
                        pltpu.VMEM((2, d, 2 * f), BF16),
                        pltpu.VMEM((rows1, 2 * f), F32),
                        pltpu.SemaphoreType.DMA((1,))],
    )
    act = pl.pallas_call(
        functools.partial(_expert_up_kernel, tm=tm, rows=rows1),
        grid_spec=up_spec,
        out_shape=jax.ShapeDtypeStruct((n_rows, f), BF16),
        compiler_params=_params("arbitrary"),
        name="experts_up",
    )(block_e, n_used, row_tok, *tables, hp, w1, b1)

    down_spec = pltpu.PrefetchScalarGridSpec(
        num_scalar_prefetch=7,
        grid=(n_blocks,),
        in_specs=[pl.BlockSpec((tm, f), rmap),
                  pl.BlockSpec(memory_space=pl.ANY),
                  pl.BlockSpec((None, 1, d), emap)],
        out_specs=pl.BlockSpec((tm, half), rmap),
        scratch_shapes=[pltpu.VMEM((2, f, d), BF16),
                        pltpu.VMEM((rows2, d), F32),
                        pltpu.SemaphoreType.DMA((1,))],
    )
    return pl.pallas_call(
        functools.partial(_expert_down_kernel, rows=rows2),
        grid_spec=down_spec,
        out_shape=jax.ShapeDtypeStruct((n_rows, half), jnp.uint32),
        compiler_params=_params("arbitrary"),
        name="experts_down",
    )(block_e, n_used, *tables, act, w2, b2)


def _combine_kernel(pos_ref, x_ref, w_ref, g_ref, y_hbm, o_ref, ybuf, sem, *, tc, top_k, normalize):
    i = pl.program_id(0)
    n = pl.num_programs(0)

    def row_copy(p, r, k, slot):
        return pltpu.make_async_copy(y_hbm.at[pl.ds(p, 1), :], ybuf.at[slot, k, pl.ds(r, 1), :],
                                     sem.at[slot])

    def issue(blk, slot):
        base = blk * (tc * top_k)

        def body(r, carry):
            for k in range(top_k):
                row_copy(pos_ref[base + r * top_k + k], r, k, slot).start()
            return carry

        lax.fori_loop(0, tc, body, 0)

    def wait(slot):
        for k in range(top_k):
            pltpu.make_async_copy(y_hbm.at[pl.ds(0, tc), :], ybuf.at[slot, k], sem.at[slot]).wait()

    @pl.when(i == 0)
    def _():
        issue(0, 0)
        issue(1, 1)

    slot = i % GATHER_SLOTS
    wait(slot)
    w = w_ref[...]
    x = x_ref[...]
    half = x.shape[1] // 2
    acc_lo = x[:, :half]
    acc_hi = x[:, half:]
    for k in range(top_k):
        lo, hi = _unpack_pairs(ybuf[slot, k])
        acc_lo = acc_lo + lo * w[:, k:k + 1]
        acc_hi = acc_hi + hi * w[:, k:k + 1]
    acc = jnp.concatenate([acc_lo, acc_hi], axis=1)
    if normalize:
        ms = jnp.mean(acc * acc, axis=-1, keepdims=True)
        acc = acc * lax.rsqrt(ms + NORM_EPS) * g_ref[...]
    o_ref[...] = acc
    ahead = (i + 2) % GATHER_SLOTS
    base = (i + 2) * (tc * top_k)
    for r in range(tc):
        for k in range(top_k):
            row_copy(pos_ref[base + r * top_k + k], r, k, ahead).start()

    @pl.when(i == n - 1)
    def _():
        wait((n % GATHER_SLOTS))
        wait(((n + 1) % GATHER_SLOTS))


def _combine(x, top_w, pos, y_rows, g, normalize, tc=128):
    t, d = x.shape
    tc = _tile(t, tc)
    assert t // tc >= 2
    pos = jnp.concatenate([pos.reshape(-1), jnp.zeros((2 * tc * TOP_K,), jnp.int32)])
    grid_spec = pltpu.PrefetchScalarGridSpec(
        num_scalar_prefetch=1,
        grid=(t // tc,),
        in_specs=[pl.BlockSpec((tc, d), lambda i, p: (i, 0)),
                  pl.BlockSpec((tc, TOP_K), lambda i, p: (i, 0)),
                  pl.BlockSpec((1, d), lambda i, p: (0, 0)),
                  pl.BlockSpec(memory_space=pl.ANY)],
        out_specs=pl.BlockSpec((tc, d), lambda i, p: (i, 0)),
        scratch_shapes=[pltpu.VMEM((GATHER_SLOTS, TOP_K, tc, d // 2), jnp.uint32),
                        pltpu.SemaphoreType.DMA((GATHER_SLOTS,))],
    )
    return pl.pallas_call(
        functools.partial(_combine_kernel, tc=tc, top_k=TOP_K, normalize=normalize),
        grid_spec=grid_spec,
        out_shape=jax.ShapeDtypeStruct((t, d), F32),
        compiler_params=_params("arbitrary"),
        name="combine",
    )(pos, x, top_w, g.reshape(1, d), y_rows)


def _moe_block_rows(n_assign):
    return 256 if n_assign >= 256 * 32 else 128


def kernel(x, mem, positions, norm_mix_g, w_in, w_s, b_s, ln_v_g, ln_v_b, beta_a, beta_b, w_out,
           norm_x_g, norm_mem_g, w_xq, w_xkv, w_xo, norm_ffn_g, w_router, b_router, w1, b1, w2, b2,
           norm_final_g):
    bsz, s, d = x.shape
    t = bsz * s
    depth = w_in.shape[0]
    chunk = w_s.shape[2]
    n_exp = w_router.shape[2]
    xs = x.reshape(t, d)
    mem2 = mem.reshape(-1, d)

    for l in range(depth):
        h = _rmsnorm(xs, norm_mix_g[l], BF16)
        proj = _matmul([h], w_in[l], BF16, name="w_in")
        out_a = _mixer_a(proj, w_s[l], b_s[l], ln_v_g[l], ln_v_b[l], beta_a[l])
        out_b = _mixer_b(proj, positions, beta_b[l], chunk, B_HEADS)
        xs = _matmul([out_a, out_b], w_out[l], F32, res=xs, name="w_out")

        h = _rmsnorm(xs, norm_x_g[l], BF16)
        mem_n = _rmsnorm(mem2, norm_mem_g[l], BF16)
        q = _matmul([h], w_xq[l], BF16, name="w_xq")
        kv = _matmul([mem_n], w_xkv[l], BF16, name="w_xkv")
        o = _xattn(q, kv, bsz, X_HEADS)
        xs = _matmul([o], w_xo[l], F32, res=xs, name="w_xo")

        hf, top_idx, top_w, rank, counts = _router(xs, norm_ffn_g[l], w_router[l], b_router[l])
        n_assign = t * TOP_K
        tm = _moe_block_rows(n_assign)
        counts = counts.reshape(n_exp).astype(jnp.int32)
        padded = (counts + tm - 1) // tm * tm
        pends = jnp.cumsum(padded)
        pstarts = pends - padded
        pos = pstarts[top_idx] + rank
        n_blocks = -(-n_assign // tm) + n_exp
        n_rows = n_blocks * tm
        row_tok = jnp.zeros((n_rows + tm,), jnp.int32).at[pos.reshape(-1)].set(
            jnp.arange(n_assign, dtype=jnp.int32) // TOP_K)
        block_start = jnp.arange(n_blocks, dtype=jnp.int32) * tm
        block_e = jnp.sum((pends[None, :] <= block_start[:, None]).astype(jnp.int32), axis=1)
        block_e = jnp.minimum(block_e, n_exp - 1)
        n_used = (pends[-1:] // tm).astype(jnp.int32)
        f = w2.shape[2]
        y_rows = _experts(hf, block_e, n_used, row_tok, w1[l], b1[l].reshape(n_exp, 1, 2 * f),
                          w2[l], b2[l].reshape(n_exp, 1, d), tm)
        xs = _combine(xs, top_w, pos, y_rows, norm_final_g, normalize=(l + 1 == depth))
    return xs.reshape(bsz, s, d)
```

```python
import functools
import math

import jax
import jax.numpy as jnp
from jax import lax
from jax.experimental import pallas as pl
from jax.experimental.pallas import tpu as pltpu

B_HEADS = 8
X_HEADS = 4
TOP_K = 4
ROPE_BASE = 10000.0
NORM_EPS = 1e-5
RET_NORM_EPS = 1e-6
SWIGLU_LIMIT = 7.0
SWIGLU_ALPHA = 1.702

LANES = 128
GATHER_SLOTS = 3
W1_SLAB_ROWS = 512
W2_LAND_LAG = 3
SLAB_COPIES = 4

V7X_VMEM_BYTES = 64 * 1024 * 1024
VMEM_LIMIT_BYTES = V7X_VMEM_BYTES - 6 * 1024 * 1024

F32 = jnp.float32
BF16 = jnp.bfloat16


def _params(*semantics):
    return pltpu.CompilerParams(
        dimension_semantics=semantics,
        vmem_limit_bytes=VMEM_LIMIT_BYTES,
        disable_bounds_checks=True,
    )


def _tile(n, want):
    t = min(n, want)
    while n % t:
        t //= 2
    return t


def _pack_pairs(v):
    half = v.shape[1] // 2
    bits = lax.bitcast_convert_type(v.astype(BF16).astype(F32), jnp.uint32)
    return (bits[:, :half] >> 16) | (bits[:, half:] & jnp.uint32(0xFFFF0000))


def _store_row_slabs(ref, first_chunk, words):
    m, w = words.shape
    r = ref.shape[0] // m
    for s in range(w // LANES):
        ref[pl.ds(first_chunk + s, m, stride=r), :] = words[:, s * LANES:(s + 1) * LANES]


def _load_row_slabs(read_rows, m, r):
    return jnp.concatenate([read_rows(pl.ds(s, m, stride=r)) for s in range(r)], axis=1)


def _unpack_pairs(p):
    lo = lax.bitcast_convert_type(p << 16, F32)
    hi = lax.bitcast_convert_type(p & jnp.uint32(0xFFFF0000), F32)
    return lo, hi


def _rms_kernel(x_ref, g_ref, o_ref):
    x = x_ref[...]
    ms = jnp.mean(x * x, axis=-1, keepdims=True)
    o_ref[...] = (x * lax.rsqrt(ms + NORM_EPS) * g_ref[...]).astype(o_ref.dtype)


def _rmsnorm(x, g, out_dtype, tm=512):
    m, d = x.shape
    tm = _tile(m, tm)
    return pl.pallas_call(
        _rms_kernel,
        grid=(m // tm,),
        in_specs=[pl.BlockSpec((tm, d), lambda i: (i, 0)),
                  pl.BlockSpec((1, d), lambda i: (0, 0))],
        out_specs=pl.BlockSpec((tm, d), lambda i: (i, 0)),
        out_shape=jax.ShapeDtypeStruct((m, d), out_dtype),
        compiler_params=_params("arbitrary"),
        name="rmsnorm",
    )(x, g.reshape(1, d))


def _mm_kernel(*refs, n_a, has_res):
    a_refs = refs[:n_a]
    w_ref = refs[n_a]
    r_ref = refs[n_a + 1] if has_res else None
    o_ref, wbf = refs[-2:]

    @pl.when(pl.program_id(1) == 0)
    def _():
        wbf[...] = w_ref[...].astype(BF16)

    acc = None
    k0 = 0
    for a_ref in a_refs:
        ka = a_ref.shape[1]
        part = jnp.dot(a_ref[...], wbf[k0:k0 + ka, :], preferred_element_type=F32)
        acc = part if acc is None else acc + part
        k0 += ka
    if has_res:
        acc = acc + r_ref[...]
    o_ref[...] = acc.astype(o_ref.dtype)


def _matmul(a_parts, w, out_dtype, res=None, tm=1024, tn=512, name="matmul"):
    m = a_parts[0].shape[0]
    k, n = w.shape
    assert sum(a.shape[1] for a in a_parts) == k
    tm = _tile(m, tm)
    tn = _tile(n, tn)
    in_specs = [pl.BlockSpec((tm, a.shape[1]), lambda j, i: (i, 0)) for a in a_parts]
    in_specs.append(pl.BlockSpec((k, tn), lambda j, i: (0, j)))
    args = list(a_parts) + [w]
    if res is not None:
        in_specs.append(pl.BlockSpec((tm, tn), lambda j, i: (i, j)))
        args.append(res)
    return pl.pallas_call(
        functools.partial(_mm_kernel, n_a=len(a_parts), has_res=res is not None),
        grid=(n // tn, m // tm),
        in_specs=in_specs,
        out_specs=pl.BlockSpec((tm, tn), lambda j, i: (i, j)),
        out_shape=jax.ShapeDtypeStruct((m, n), out_dtype),
        scratch_shapes=[pltpu.VMEM((k, tn), BF16)],
        compiler_params=_params("arbitrary", "arbitrary"),
        name=name,
    )(*args)


def _amix_kernel(a_ref, ws_ref, bst_ref, lng_ref, lnb_ref, beta_ref, o_ref, *, chunk, heads, hd):
    aw = heads * hd
    a = a_ref[...].astype(F32)
    hid = 0.5 * a * (1.0 + lax.erf(a * (2.0 ** -0.5)))
    u = hid[:, :aw]
    v = hid[:, aw:]
    mean = jnp.mean(v, axis=-1, keepdims=True)
    vc = v - mean
    var = jnp.mean(vc * vc, axis=-1, keepdims=True)
    vn = (vc * lax.rsqrt(var + NORM_EPS) * lng_ref[...] + lnb_ref[...]).astype(BF16)
    ub = u * beta_ref[...]
    rows = a.shape[0]
    row = lax.broadcasted_iota(jnp.int32, (chunk, chunk), 0)
    col = lax.broadcasted_iota(jnp.int32, (chunk, chunk), 1)
    causal = row >= col
    bst = bst_ref[...]
    for h in range(heads):
        w = jnp.where(causal, ws_ref[h], 0.0).astype(BF16)
        bias = bst[:, h:h + 1]
        for c in range(rows // chunk):
            rs = slice(c * chunk, (c + 1) * chunk)
            cs = slice(h * hd, (h + 1) * hd)
            mixed = jnp.dot(w, vn[rs, cs], preferred_element_type=F32) + bias
            o_ref[rs, cs] = (ub[rs, cs] * mixed).astype(o_ref.dtype)


def _mixer_a(proj, w_s, b_s, ln_g, ln_b, beta, rows=256):
    t = proj.shape[0]
    heads, chunk, _ = w_s.shape
    aw = ln_g.shape[0]
    hd = aw // heads
    rows = _tile(t, rows)
    assert rows % chunk == 0
    kern = functools.partial(_amix_kernel, chunk=chunk, heads=heads, hd=hd)
    return pl.pallas_call(
        kern,
        grid=(t // rows,),
        in_specs=[pl.BlockSpec((rows, 2 * aw), lambda i: (i, 0)),
                  pl.BlockSpec((heads, chunk, chunk), lambda i: (0, 0, 0)),
                  pl.BlockSpec((chunk, heads), lambda i: (0, 0)),
                  pl.BlockSpec((1, aw), lambda i: (0, 0)),
                  pl.BlockSpec((1, aw), lambda i: (0, 0)),
                  pl.BlockSpec((1, aw), lambda i: (0, 0))],
        out_specs=pl.BlockSpec((rows, aw), lambda i: (i, 0)),
        out_shape=jax.ShapeDtypeStruct((t, aw), BF16),
        compiler_params=_params("arbitrary"),
        name="mixer_a",
    )(proj, w_s, b_s.T, ln_g.reshape(1, aw), ln_b.reshape(1, aw), beta.reshape(1, aw))


def _ret_kernel(pos_ref, invf_ref, din_ref, qd_ref, kd_ref, q_ref, k_ref, v_ref, g_ref, beta_ref,
                o_ref, state, *, heads, hd, chunk):
    @pl.when(pl.program_id(1) == 0)
    def _():
        state[...] = jnp.zeros_like(state)

    half = hd // 2
    ang = pos_ref[...] * invf_ref[...]
    cos = jnp.cos(ang)
    sin = jnp.sin(ang)
    qd_all = qd_ref[...]
    kd_all = kd_ref[...]
    k_scale = hd ** -0.5

    def rot(t):
        t1 = t[:, :half]
        t2 = t[:, half:]
        return jnp.concatenate([t1 * cos - t2 * sin, t2 * cos + t1 * sin], axis=-1)

    for h in range(heads):
        cs = slice(h * hd, (h + 1) * hd)
        chunk_decay = math.exp(math.log(1.0 - 2.0 ** (-5.0 - h)) * chunk)
        qr = rot(q_ref[:, cs].astype(F32))
        kr = rot(k_ref[:, cs].astype(F32)) * k_scale
        vb = v_ref[:, cs]
        scores = lax.dot_general(qr.astype(BF16), kr.astype(BF16), (((1,), (1,)), ((), ())),
                                 preferred_element_type=F32) * din_ref[h]
        inner = jnp.dot(scores.astype(BF16), vb, preferred_element_type=F32)
        st = state[h]
        cross = jnp.dot((qr * qd_all[:, h:h + 1]).astype(BF16), st.astype(BF16),
                        preferred_element_type=F32)
        kv = lax.dot_general((kr * kd_all[:, h:h + 1]).astype(BF16), vb, (((0,), (0,)), ((), ())),
                             preferred_element_type=F32)
        state[h] = st * chunk_decay + kv
        out = inner + cross
        out = out * lax.rsqrt(jnp.mean(out * out, axis=-1, keepdims=True) + RET_NORM_EPS)
        g = g_ref[:, cs].astype(F32)
        out = out * (g * jax.nn.sigmoid(g)) * beta_ref[:, cs]
        o_ref[:, cs] = out.astype(o_ref.dtype)


def _mixer_b(proj, positions, beta, chunk, heads):
    t = proj.shape[0]
    bsz, s = positions.shape
    bw = beta.shape[0]
    hd = bw // heads
    half = hd // 2
    n_chunks = s // chunk
    assert proj.shape[1] == 6 * bw
    log_gamma = jnp.log(1.0 - 2.0 ** (-5.0 - jnp.arange(heads, dtype=F32)))
    idx = jnp.arange(chunk, dtype=F32)
    diff = idx[:, None] - idx[None, :]
    decay_in = jnp.where(diff >= 0, jnp.exp(log_gamma[:, None, None] * jnp.maximum(diff, 0.0)), 0.0)
    q_decay = jnp.exp(log_gamma[None, :] * (idx[:, None] + 1.0))
    k_decay = jnp.exp(log_gamma[None, :] * (chunk - 1.0 - idx[:, None]))
    inv_freq = (ROPE_BASE ** (-jnp.arange(half, dtype=F32) / half)).reshape(1, half)
    pos = positions.astype(F32).reshape(t, 1)

    kern = functools.partial(_ret_kernel, heads=heads, hd=hd, chunk=chunk)
    row_map = lambda b, c: (b * n_chunks + c, 0)
    const2 = lambda b, c: (0, 0)

    def col_block(jb):
        return pl.BlockSpec((chunk, bw), lambda b, c: (b * n_chunks + c, jb))

    return pl.pallas_call(
        kern,
        grid=(bsz, n_chunks),
        in_specs=[pl.BlockSpec((chunk, 1), row_map),
                  pl.BlockSpec((1, half), const2),
                  pl.BlockSpec((heads, chunk, chunk), lambda b, c: (0, 0, 0)),
                  pl.BlockSpec((chunk, heads), const2),
                  pl.BlockSpec((chunk, heads), const2),
                  col_block(2), col_block(3), col_block(4), col_block(5),
                  pl.BlockSpec((1, bw), const2)],
        out_specs=pl.BlockSpec((chunk, bw), row_map),
        out_shape=jax.ShapeDtypeStruct((t, bw), BF16),
        scratch_shapes=[pltpu.VMEM((heads, hd, hd), F32)],
        compiler_params=_params("arbitrary", "arbitrary"),
        name="mixer_b",
    )(pos, inv_freq, decay_in, q_decay, k_decay, proj, proj, proj, proj, beta.reshape(1, bw))


def _xattn_kernel(q_ref, k_ref, v_ref, o_ref, *, scale):
    s = lax.dot_general(q_ref[...], k_ref[...], (((1,), (1,)), ((), ())),
                        preferred_element_type=F32) * scale
    m = jnp.max(s, axis=-1, keepdims=True)
    p = jnp.exp(s - m)
    l = jnp.sum(p, axis=-1, keepdims=True)
    o = jnp.dot(p.astype(BF16), v_ref[...], preferred_element_type=F32)
    o_ref[...] = (o / l).astype(o_ref.dtype)


def _xattn(q, kv, bsz, heads, tq=512):
    t, d = q.shape
    s = t // bsz
    m = kv.shape[0] // bsz
    dh = d // heads
    tq = _tile(s, tq)
    nq = s // tq
    return pl.pallas_call(
        functools.partial(_xattn_kernel, scale=dh ** -0.5),
        grid=(bsz, nq, heads),
        in_specs=[pl.BlockSpec((tq, dh), lambda b, i, h: (b * nq + i, h)),
                  pl.BlockSpec((m, dh), lambda b, i, h: (b, h)),
                  pl.BlockSpec((m, dh), lambda b, i, h: (b, heads + h))],
        out_specs=pl.BlockSpec((tq, dh), lambda b, i, h: (b * nq + i, h)),
        out_shape=jax.ShapeDtypeStruct((t, d), BF16),
        compiler_params=_params("arbitrary", "arbitrary", "arbitrary"),
        name="xattn",
    )(q, kv, kv)


def _router_kernel(x_ref, g_ref, wr_ref, br_ref, h_ref, idx_ref, w_ref, rank_ref, cnt_ref, cnt, *,
                   n_exp, top_k):
    @pl.when(pl.program_id(0) == 0)
    def _():
        cnt[...] = jnp.zeros_like(cnt)

    x = x_ref[...]
    ms = jnp.mean(x * x, axis=-1, keepdims=True)
    h = x * lax.rsqrt(ms + NORM_EPS) * g_ref[...]
    _store_row_slabs(h_ref, 0, _pack_pairs(h))
    wr = wr_ref[...]
    h_hi = h.astype(BF16)
    h_lo = (h - h_hi.astype(F32)).astype(BF16)
    w_hi = wr.astype(BF16)
    w_lo = (wr - w_hi.astype(F32)).astype(BF16)
    logits = (jnp.dot(h_hi, w_hi, preferred_element_type=F32)
              + jnp.dot(h_hi, w_lo, preferred_element_type=F32)
              + jnp.dot(h_lo, w_hi, preferred_element_type=F32)) + br_ref[...]
    tt = x.shape[0]
    lane = lax.broadcasted_iota(jnp.int32, (tt, n_exp), 1)
    work = logits
    vals, idxs = [], []
    onehot = jnp.zeros((tt, n_exp), F32)
    for _ in range(top_k):
        mx = jnp.max(work, axis=-1, keepdims=True)
        ix = jnp.min(jnp.where(work == mx, lane, n_exp), axis=-1, keepdims=True)
        sel = lane == ix
        onehot = jnp.where(sel, 1.0, onehot)
        work = jnp.where(sel, -jnp.inf, work)
        vals.append(mx)
        idxs.append(ix)
    exps = [jnp.exp(v - vals[0]) for v in vals]
    denom = exps[0]
    for e in exps[1:]:
        denom = denom + e
    r = lax.broadcasted_iota(jnp.int32, (tt, tt), 0)
    c = lax.broadcasted_iota(jnp.int32, (tt, tt), 1)
    lower = jnp.where(r > c, 1.0, 0.0).astype(BF16)
    before = jnp.dot(lower, onehot.astype(BF16), preferred_element_type=F32) + cnt[...]
    for k in range(top_k):
        idx_ref[:, k:k + 1] = idxs[k]
        w_ref[:, k:k + 1] = exps[k] / denom
        rk = jnp.sum(jnp.where(lane == idxs[k], before, 0.0), axis=-1, keepdims=True)
        rank_ref[:, k:k + 1] = rk.astype(jnp.int32)
    cnt[...] = cnt[...] + jnp.sum(onehot, axis=0, keepdims=True)
    cnt_ref[...] = cnt[...]


def _router(x, g, w_router, b_router, tt=256):
    t, d = x.shape
    n_exp = w_router.shape[1]
    tt = _tile(t, tt)
    kern = functools.partial(_router_kernel, n_exp=n_exp, top_k=TOP_K)
    tok_map = lambda i: (i, 0)
    const = lambda i: (0, 0)
    return pl.pallas_call(
        kern,
        grid=(t // tt,),
        in_specs=[pl.BlockSpec((tt, d), tok_map),
                  pl.BlockSpec((1, d), const),
                  pl.BlockSpec((d, n_exp), const),
                  pl.BlockSpec((1, n_exp), const)],
        out_specs=[pl.BlockSpec((tt * (d // 2 // LANES), LANES), tok_map),
                   pl.BlockSpec((tt, TOP_K), tok_map),
                   pl.BlockSpec((tt, TOP_K), tok_map),
                   pl.BlockSpec((tt, TOP_K), tok_map),
                   pl.BlockSpec((1, n_exp), const)],
        out_shape=[jax.ShapeDtypeStruct((t * (d // 2 // LANES), LANES), jnp.uint32),
                   jax.ShapeDtypeStruct((t, TOP_K), jnp.int32),
                   jax.ShapeDtypeStruct((t, TOP_K), F32),
                   jax.ShapeDtypeStruct((t, TOP_K), jnp.int32),
                   jax.ShapeDtypeStruct((1, n_exp), F32)],
        scratch_shapes=[pltpu.VMEM((1, n_exp), F32)],
        compiler_params=_params("arbitrary"),
        name="router",
    )(x, g.reshape(1, d), w_router, b_router.reshape(1, n_exp))


def _column_groups(width):
    return max(1, min(4, width // (2 * LANES)))


def _run_tables(block_e, n_used):
    n = block_e.shape[0]
    i = jnp.arange(n, dtype=jnp.int32)
    used = i < n_used[0]
    prev_e = jnp.concatenate([jnp.full((1,), -1, jnp.int32), block_e[:-1]])
    first = used & ((i == 0) | (block_e != prev_e))
    slot = (jnp.cumsum(first.astype(jnp.int32)) - 1) % 2
    jrun = i - lax.cummax(jnp.where(first, i, 0))
    first_pos = jnp.where(first, i, n)
    next_first = jnp.concatenate([lax.cummin(first_pos, reverse=True)[1:], jnp.full((1,), n, jnp.int32)])
    nxt = jnp.where(next_first < n, block_e[jnp.minimum(next_first, n - 1)], -1)
    plen = jnp.concatenate([jnp.zeros((1,), jnp.int32), jrun[:-1] + 1])
    as_i32 = lambda a: a.astype(jnp.int32)
    return as_i32(first), as_i32(slot), as_i32(jrun), as_i32(nxt), as_i32(plen)


def _weight_streamer(i, be_ref, first_ref, slot_ref, jrun_ref, nxt_ref, plen_ref,
                     w_hbm, wres, stage, wsem, rows):
    nslab = w_hbm.shape[1] // rows
    e = be_ref[i]
    slot = slot_ref[i]
    j = jrun_ref[i]
    nxt = nxt_ref[i]
    sem = wsem.at[0]

    def fetch(expert, c):
        sub = rows // SLAB_COPIES
        return [pltpu.make_async_copy(w_hbm.at[expert, pl.ds(c * rows + p * sub, sub), :],
                                      stage.at[pl.ds(p * sub, sub), :], sem)
                for p in range(SLAB_COPIES)]

    def cast_into(dst, c):
        wres[dst, pl.ds(pl.multiple_of(c * rows, rows), rows), :] = stage[...].astype(BF16)

    def ensure():
        @pl.when(first_ref[i] == 1)
        def _():
            done = jnp.where(i == 0, 0, jnp.minimum(plen_ref[i], nslab))

            def body(c, carry):
                copies = fetch(e, c)
                for cp in copies:
                    cp.start()
                for cp in copies:
                    cp.wait()
                cast_into(slot, c)
                return carry

            lax.fori_loop(done, nslab, body, 0)

    prefetching = jnp.logical_and(j < nslab, nxt >= 0)

    def start():
        @pl.when(prefetching)
        def _():
            for cp in fetch(nxt, j):
                cp.start()

    def finish():
        @pl.when(prefetching)
        def _():
            for cp in fetch(nxt, j):
                cp.wait()
            cast_into(1 - slot, j)

    return slot, ensure, start, finish


def _whole_expert_streamer(i, be_ref, first_ref, slot_ref, jrun_ref, nxt_ref, plen_ref,
                           w_hbm, wres, stage, wsem, lag):
    e = be_ref[i]
    slot = slot_ref[i]
    j = jrun_ref[i]
    nxt = nxt_ref[i]
    sem = wsem.at[0]
    sub = stage.shape[0] // SLAB_COPIES

    def fetch(expert):
        return [pltpu.make_async_copy(w_hbm.at[expert, pl.ds(p * sub, sub), :],
                                      stage.at[pl.ds(p * sub, sub), :], sem)
                for p in range(SLAB_COPIES)]

    def land(dst):
        for cp in fetch(e):
            cp.wait()
        wres[dst] = stage[...].astype(BF16)

    def ensure():
        @pl.when(first_ref[i] == 1)
        def _():
            @pl.when(i == 0)
            def _():
                for cp in fetch(e):
                    cp.start()
                land(slot)

            @pl.when(jnp.logical_and(i > 0, plen_ref[i] <= lag))
            def _():
                land(slot)

    def start():
        @pl.when(jnp.logical_and(j == 0, nxt >= 0))
        def _():
            for cp in fetch(nxt):
                cp.start()

    def finish():
        @pl.when(jnp.logical_and(j == lag, nxt >= 0))
        def _():
            land(1 - slot)

    return slot, ensure, start, finish


def _expert_up_kernel(be_ref, nb_ref, tok_ref, first_ref, slot_ref, jrun_ref, nxt_ref, plen_ref,
                      h_hbm, w1_hbm, b1_ref, o_ref, xbuf, sem, wres, stage, wsem, *, tm, rows):
    i = pl.program_id(0)
    nb = nb_ref[0]

    rpt = xbuf.shape[1] // tm

    def row_copy(tok, r, slot):
        return pltpu.make_async_copy(h_hbm.at[pl.ds(tok * rpt, rpt), :],
                                     xbuf.at[slot, pl.ds(r * rpt, rpt), :], sem.at[slot])

    def issue(blk, slot):
        base = blk * tm

        def body(r, carry):
            row_copy(tok_ref[base + r], r, slot).start()
            return carry

        lax.fori_loop(0, tm, body, 0)

    def wait(slot):
        pltpu.make_async_copy(h_hbm.at[pl.ds(0, tm * rpt), :], xbuf.at[slot], sem.at[slot]).wait()

    @pl.when(i == 0)
    def _():
        issue(0, 0)
        issue(1, 1)

    @pl.when(i < nb)
    def _():
        wslot, ensure_weights, prefetch_start, prefetch_finish = _weight_streamer(
            i, be_ref, first_ref, slot_ref, jrun_ref, nxt_ref, plen_ref, w1_hbm, wres, stage, wsem, rows)
        ensure_weights()
        prefetch_start()
        cur = i % GATHER_SLOTS
        wait(cur)
        lo, hi = _unpack_pairs(_load_row_slabs(lambda rows_: xbuf[cur, rows_, :], tm, rpt))
        ahead = (i + 2) % GATHER_SLOTS
        base = (i + 2) * tm
        half = lo.shape[1]
        lo_b = lo.astype(BF16)
        hi_b = hi.astype(BF16)
        f = o_ref.shape[1]
        lane = lax.broadcasted_iota(jnp.int32, (tm, LANES), 1)
        first = lane < LANES // 2
        idx_glu = (2 * lane) % LANES
        idx_lin = (2 * lane + 1) % LANES
        n_groups = _column_groups(f)
        gw = 2 * f // n_groups
        start_groups = max(1, n_groups - 1)
        for q in range(n_groups):
            share = range(q * tm // start_groups, (q + 1) * tm // start_groups) if q < start_groups else ()
            for r in share:
                row_copy(tok_ref[base + r], r, ahead).start()
            pin = lax.bitcast_convert_type(xbuf[cur, 0:8, 0:LANES], F32)
            cols = slice(q * gw, (q + 1) * gw)
            hh = (jnp.dot(lo_b, wres[wslot, :half, cols], preferred_element_type=F32)
                  + jnp.dot(hi_b, wres[wslot, half:, cols], preferred_element_type=F32)
                  + b1_ref[:, cols])
            acts = []
            for c in range(gw // (2 * LANES)):
                v0 = hh[:, 2 * c * LANES:(2 * c + 1) * LANES]
                v1 = hh[:, (2 * c + 1) * LANES:(2 * c + 2) * LANES]
                g = jnp.where(first, jnp.take_along_axis(v0, idx_glu, axis=1),
                              jnp.take_along_axis(v1, idx_glu, axis=1))
                li = jnp.where(first, jnp.take_along_axis(v0, idx_lin, axis=1),
                               jnp.take_along_axis(v1, idx_lin, axis=1))
                glu = jnp.minimum(g, SWIGLU_LIMIT)
                lin = jnp.clip(li, -SWIGLU_LIMIT, SWIGLU_LIMIT)
                act = glu * jax.nn.sigmoid(SWIGLU_ALPHA * glu) * (lin + 1.0)
                if c == 0:
                    never = glu[0:8] > SWIGLU_LIMIT
                    act = jnp.concatenate([jnp.where(never, pin, act[0:8]), act[8:]], axis=0)
                acts.append(act.astype(BF16))
            o_ref[:, q * (gw // 2):(q + 1) * (gw // 2)] = jnp.concatenate(acts, axis=1)
        prefetch_finish()

    @pl.when(i == nb)
    def _():
        wait(nb % GATHER_SLOTS)
        wait((nb + 1) % GATHER_SLOTS)

    @pl.when(i >= nb)
    def _():
        o_ref[...] = jnp.zeros_like(o_ref)


def _expert_down_kernel(be_ref, nb_ref, first_ref, slot_ref, jrun_ref, nxt_ref, plen_ref,
                        a_ref, w2_hbm, b2_ref, o_ref, wres, stage, wsem):
    i = pl.program_id(0)
    nb = nb_ref[0]

    @pl.when(i < nb)
    def _():
        wslot, ensure_weights, prefetch_start, prefetch_finish = _whole_expert_streamer(
            i, be_ref, first_ref, slot_ref, jrun_ref, nxt_ref, plen_ref, w2_hbm, wres, stage, wsem,
            W2_LAND_LAG)
        ensure_weights()
        prefetch_start()
        a = a_ref[...]
        half = wres.shape[2] // 2
        gw = half // _column_groups(half)
        for q in range(half // gw):
            cl = slice(q * gw, (q + 1) * gw)
            ch = slice(half + q * gw, half + (q + 1) * gw)
            y_lo = jnp.dot(a, wres[wslot, :, cl], preferred_element_type=F32) + b2_ref[:, cl]
            y_hi = jnp.dot(a, wres[wslot, :, ch], preferred_element_type=F32) + b2_ref[:, ch]
            o_ref[:, cl] = _pack_pairs(jnp.concatenate([y_lo, y_hi], axis=1))
        prefetch_finish()

    @pl.when(i >= nb)
    def _():
        o_ref[...] = jnp.zeros_like(o_ref)


def _experts(hp, block_e, n_used, row_tok, w1, b1, w2, b2, tm):
    n_exp, f, d = w2.shape
    half = d // 2
    rpt = half // LANES
    t = hp.shape[0] // rpt
    assert f % (2 * LANES) == 0 and hp.shape == (t * rpt, LANES)
    n_blocks = row_tok.shape[0] // tm - 1
    n_rows = n_blocks * tm
    assert (t * TOP_K) % tm == 0 and n_blocks == (t * TOP_K) // tm + n_exp
    tables = _run_tables(block_e, n_used)
    rows1 = _tile(d, W1_SLAB_ROWS)

    def emap(i, be, *_):
        return (be[i], 0, 0)

    def rmap(i, *_):
        return (i, 0)

    up_spec = pltpu.PrefetchScalarGridSpec(
        num_scalar_prefetch=8,
        grid=(n_blocks,),
        in_specs=[pl.BlockSpec(memory_space=pl.ANY),
                  pl.BlockSpec(memory_space=pl.ANY),
                  pl.BlockSpec((None, 1, 2 * f), emap)],
        out_specs=pl.BlockSpec((tm, f), rmap),
        scratch_shapes=[pltpu.VMEM((GATHER_SLOTS, tm * rpt, LANES), jnp.uint32),
                        pltpu.SemaphoreType.DMA((GATHER_SLOTS,)),
                        pltpu.VMEM((2, d, 2 * f), BF16),
                        pltpu.VMEM((rows1, 2 * f), F32),
                        pltpu.SemaphoreType.DMA((1,))],
    )
    act = pl.pallas_call(
        functools.partial(_expert_up_kernel, tm=tm, rows=rows1),
        grid_spec=up_spec,
        out_shape=jax.ShapeDtypeStruct((n_rows, f), BF16),
        compiler_params=_params("arbitrary"),
        name="experts_up",
    )(block_e, n_used, row_tok, *tables, hp, w1, b1)

    down_spec = pltpu.PrefetchScalarGridSpec(
        num_scalar_prefetch=7,
        grid=(n_blocks,),
        in_specs=[pl.BlockSpec((tm, f), rmap),
                  pl.BlockSpec(memory_space=pl.ANY),
                  pl.BlockSpec((None, 1, d), emap)],
        out_specs=pl.BlockSpec((tm, half), rmap),
        scratch_shapes=[pltpu.VMEM((2, f, d), BF16),
                        pltpu.VMEM((f, d), F32),
                        pltpu.SemaphoreType.DMA((1,))],
    )
    return pl.pallas_call(
        _expert_down_kernel,
        grid_spec=down_spec,
        out_shape=jax.ShapeDtypeStruct((n_rows, half), jnp.uint32),
        compiler_params=_params("arbitrary"),
        name="experts_down",
    )(block_e, n_used, *tables, act, w2, b2)


def _combine_kernel(pos_ref, x_ref, w_ref, g_ref, y_hbm, o_ref, ybuf, sem, *, tc, top_k, normalize):
    i = pl.program_id(0)
    n = pl.num_programs(0)

    def row_copy(p, r, k, slot):
        return pltpu.make_async_copy(y_hbm.at[pl.ds(p, 1), :], ybuf.at[slot, k, pl.ds(r, 1), :],
                                     sem.at[slot])

    def issue(blk, slot):
        base = blk * (tc * top_k)

        def body(r, carry):
            for k in range(top_k):
                row_copy(pos_ref[base + r * top_k + k], r, k, slot).start()
            return carry

        lax.fori_loop(0, tc, body, 0)

    def wait(slot):
        for k in range(top_k):
            pltpu.make_async_copy(y_hbm.at[pl.ds(0, tc), :], ybuf.at[slot, k], sem.at[slot]).wait()

    @pl.when(i == 0)
    def _():
        issue(0, 0)
        issue(1, 1)

    slot = i % GATHER_SLOTS
    wait(slot)
    w = w_ref[...]
    x = x_ref[...]
    half = x.shape[1] // 2
    acc_lo = x[:, :half]
    acc_hi = x[:, half:]
    for k in range(top_k):
        lo, hi = _unpack_pairs(ybuf[slot, k])
        acc_lo = acc_lo + lo * w[:, k:k + 1]
        acc_hi = acc_hi + hi * w[:, k:k + 1]
    acc = jnp.concatenate([acc_lo, acc_hi], axis=1)
    if normalize:
        ms = jnp.mean(acc * acc, axis=-1, keepdims=True)
        acc = acc * lax.rsqrt(ms + NORM_EPS) * g_ref[...]
    o_ref[...] = acc
    ahead = (i + 2) % GATHER_SLOTS
    base = (i + 2) * (tc * top_k)
    for r in range(tc):
        for k in range(top_k):
            row_copy(pos_ref[base + r * top_k + k], r, k, ahead).start()

    @pl.when(i == n - 1)
    def _():
        wait((n % GATHER_SLOTS))
        wait(((n + 1) % GATHER_SLOTS))


def _combine(x, top_w, pos, y_rows, g, normalize, tc=128):
    t, d = x.shape
    tc = _tile(t, tc)
    assert t // tc >= 2
    pos = jnp.concatenate([pos.reshape(-1), jnp.zeros((2 * tc * TOP_K,), jnp.int32)])
    grid_spec = pltpu.PrefetchScalarGridSpec(
        num_scalar_prefetch=1,
        grid=(t // tc,),
        in_specs=[pl.BlockSpec((tc, d), lambda i, p: (i, 0)),
                  pl.BlockSpec((tc, TOP_K), lambda i, p: (i, 0)),
                  pl.BlockSpec((1, d), lambda i, p: (0, 0)),
                  pl.BlockSpec(memory_space=pl.ANY)],
        out_specs=pl.BlockSpec((tc, d), lambda i, p: (i, 0)),
        scratch_shapes=[pltpu.VMEM((GATHER_SLOTS, TOP_K, tc, d // 2), jnp.uint32),
                        pltpu.SemaphoreType.DMA((GATHER_SLOTS,))],
    )
    return pl.pallas_call(
        functools.partial(_combine_kernel, tc=tc, top_k=TOP_K, normalize=normalize),
        grid_spec=grid_spec,
        out_shape=jax.ShapeDtypeStruct((t, d), F32),
        compiler_params=_params("arbitrary"),
        name="combine",
    )(pos, x, top_w, g.reshape(1, d), y_rows)


def _moe_block_rows(n_assign):
    return 256 if n_assign >= 256 * 32 else 128


def kernel(x, mem, positions, norm_mix_g, w_in, w_s, b_s, ln_v_g, ln_v_b, beta_a, beta_b, w_out,
           norm_x_g, norm_mem_g, w_xq, w_xkv, w_xo, norm_ffn_g, w_router, b_router, w1, b1, w2, b2,
           norm_final_g):
    bsz, s, d = x.shape
    t = bsz * s
    depth = w_in.shape[0]
    chunk = w_s.shape[2]
    n_exp = w_router.shape[2]
    xs = x.reshape(t, d)
    mem2 = mem.reshape(-1, d)

    for l in range(depth):
        h = _rmsnorm(xs, norm_mix_g[l], BF16)
        proj = _matmul([h], w_in[l], BF16, name="w_in")
        out_a = _mixer_a(proj, w_s[l], b_s[l], ln_v_g[l], ln_v_b[l], beta_a[l])
        out_b = _mixer_b(proj, positions, beta_b[l], chunk, B_HEADS)
        xs = _matmul([out_a, out_b], w_out[l], F32, res=xs, name="w_out")

        h = _rmsnorm(xs, norm_x_g[l], BF16)
        mem_n = _rmsnorm(mem2, norm_mem_g[l], BF16)
        q = _matmul([h], w_xq[l], BF16, name="w_xq")
        kv = _matmul([mem_n], w_xkv[l], BF16, name="w_xkv")
        o = _xattn(q, kv, bsz, X_HEADS)
        xs = _matmul([o], w_xo[l], F32, res=xs, name="w_xo")

        hf, top_idx, top_w, rank, counts = _router(xs, norm_ffn_g[l], w_router[l], b_router[l])
        n_assign = t * TOP_K
        tm = _moe_block_rows(n_assign)
        counts = counts.reshape(n_exp).astype(jnp.int32)
        padded = (counts + tm - 1) // tm * tm
        pends = jnp.cumsum(padded)
        pstarts = pends - padded
        pos = pstarts[top_idx] + rank
        n_blocks = -(-n_assign // tm) + n_exp
        n_rows = n_blocks * tm
        row_tok = jnp.zeros((n_rows + tm,), jnp.int32).at[pos.reshape(-1)].set(
            jnp.arange(n_assign, dtype=jnp.int32) // TOP_K)
        block_start = jnp.arange(n_blocks, dtype=jnp.int32) * tm
        block_e = jnp.sum((pends[None, :] <= block_start[:, None]).astype(jnp.int32), axis=1)
        block_e = jnp.minimum(block_e, n_exp - 1)
        n_used = (pends[-1:] // tm).astype(jnp.int32)
        f = w2.shape[2]
        y_rows = _experts(hf, block_e, n_used, row_tok, w1[l], b1[l].reshape(n_exp, 1, 2 * f),
                          w2[l], b2[l].reshape(n_exp, 1, d), tm)
        xs = _combine(xs, top_w, pos, y_rows, norm_final_g, normalize=(l + 1 == depth))
    return xs.reshape(bsz, s, d)
```

```python
import functools
import math

import jax
import jax.numpy as jnp
from jax import lax
from jax.experimental import pallas as pl
from jax.experimental.pallas import tpu as pltpu

B_HEADS = 8
X_HEADS = 4
TOP_K = 4
ROPE_BASE = 10000.0
NORM_EPS = 1e-5
RET_NORM_EPS = 1e-6
SWIGLU_LIMIT = 7.0
SWIGLU_ALPHA = 1.702

LANES = 128
GATHER_SLOTS = 3
W1_SLAB_ROWS = 512
W2_LAND_LAG = 3
SLAB_COPIES = 4

V7X_VMEM_BYTES = 64 * 1024 * 1024
VMEM_LIMIT_BYTES = V7X_VMEM_BYTES - 6 * 1024 * 1024

F32 = jnp.float32
BF16 = jnp.bfloat16


def _params(*semantics):
    return pltpu.CompilerParams(
        dimension_semantics=semantics,
        vmem_limit_bytes=VMEM_LIMIT_BYTES,
        disable_bounds_checks=True,
    )


def _tile(n, want):
    t = min(n, want)
    while n % t:
        t //= 2
    return t


def _pack_pairs(v):
    half = v.shape[1] // 2
    bits = lax.bitcast_convert_type(v.astype(BF16).astype(F32), jnp.uint32)
    return (bits[:, :half] >> 16) | (bits[:, half:] & jnp.uint32(0xFFFF0000))


def _store_row_slabs(ref, first_chunk, words):
    m, w = words.shape
    r = ref.shape[0] // m
    for s in range(w // LANES):
        ref[pl.ds(first_chunk + s, m, stride=r), :] = words[:, s * LANES:(s + 1) * LANES]


def _load_row_slabs(read_rows, m, r):
    return jnp.concatenate([read_rows(pl.ds(s, m, stride=r)) for s in range(r)], axis=1)


def _unpack_pairs(p):
    lo = lax.bitcast_convert_type(p << 16, F32)
    hi = lax.bitcast_convert_type(p & jnp.uint32(0xFFFF0000), F32)
    return lo, hi


def _rms_kernel(x_ref, g_ref, o_ref):
    x = x_ref[...]
    ms = jnp.mean(x * x, axis=-1, keepdims=True)
    o_ref[...] = (x * lax.rsqrt(ms + NORM_EPS) * g_ref[...]).astype(o_ref.dtype)


def _rmsnorm(x, g, out_dtype, tm=512):
    m, d = x.shape
    tm = _tile(m, tm)
    return pl.pallas_call(
        _rms_kernel,
        grid=(m // tm,),
        in_specs=[pl.BlockSpec((tm, d), lambda i: (i, 0)),
                  pl.BlockSpec((1, d), lambda i: (0, 0))],
        out_specs=pl.BlockSpec((tm, d), lambda i: (i, 0)),
        out_shape=jax.ShapeDtypeStruct((m, d), out_dtype),
        compiler_params=_params("arbitrary"),
        name="rmsnorm",
    )(x, g.reshape(1, d))


def _mm_kernel(*refs, n_a, has_res):
    a_refs = refs[:n_a]
    w_ref = refs[n_a]
    r_ref = refs[n_a + 1] if has_res else None
    o_ref, wbf = refs[-2:]

    @pl.when(pl.program_id(1) == 0)
    def _():
        wbf[...] = w_ref[...].astype(BF16)

    acc = None
    k0 = 0
    for a_ref in a_refs:
        ka = a_ref.shape[1]
        part = jnp.dot(a_ref[...], wbf[k0:k0 + ka, :], preferred_element_type=F32)
        acc = part if acc is None else acc + part
        k0 += ka
    if has_res:
        acc = acc + r_ref[...]
    o_ref[...] = acc.astype(o_ref.dtype)


def _matmul(a_parts, w, out_dtype, res=None, tm=1024, tn=512, name="matmul"):
    m = a_parts[0].shape[0]
    k, n = w.shape
    assert sum(a.shape[1] for a in a_parts) == k
    tm = _tile(m, tm)
    tn = _tile(n, tn)
    in_specs = [pl.BlockSpec((tm, a.shape[1]), lambda j, i: (i, 0)) for a in a_parts]
    in_specs.append(pl.BlockSpec((k, tn), lambda j, i: (0, j)))
    args = list(a_parts) + [w]
    if res is not None:
        in_specs.append(pl.BlockSpec((tm, tn), lambda j, i: (i, j)))
        args.append(res)
    return pl.pallas_call(
        functools.partial(_mm_kernel, n_a=len(a_parts), has_res=res is not None),
        grid=(n // tn, m // tm),
        in_specs=in_specs,
        out_specs=pl.BlockSpec((tm, tn), lambda j, i: (i, j)),
        out_shape=jax.ShapeDtypeStruct((m, n), out_dtype),
        scratch_shapes=[pltpu.VMEM((k, tn), BF16)],
        compiler_params=_params("arbitrary", "arbitrary"),
        name=name,
    )(*args)


def _amix_kernel(a_ref, ws_ref, bst_ref, lng_ref, lnb_ref, beta_ref, o_ref, *, chunk, heads, hd):
    aw = heads * hd
    a = a_ref[...].astype(F32)
    hid = 0.5 * a * (1.0 + lax.erf(a * (2.0 ** -0.5)))
    u = hid[:, :aw]
    v = hid[:, aw:]
    mean = jnp.mean(v, axis=-1, keepdims=True)
    vc = v - mean
    var = jnp.mean(vc * vc, axis=-1, keepdims=True)
    vn = (vc * lax.rsqrt(var + NORM_EPS) * lng_ref[...] + lnb_ref[...]).astype(BF16)
    ub = u * beta_ref[...]
    rows = a.shape[0]
    row = lax.broadcasted_iota(jnp.int32, (chunk, chunk), 0)
    col = lax.broadcasted_iota(jnp.int32, (chunk, chunk), 1)
    causal = row >= col
    bst = bst_ref[...]
    for h in range(heads):
        w = jnp.where(causal, ws_ref[h], 0.0).astype(BF16)
        bias = bst[:, h:h + 1]
        for c in range(rows // chunk):
            rs = slice(c * chunk, (c + 1) * chunk)
            cs = slice(h * hd, (h + 1) * hd)
            mixed = jnp.dot(w, vn[rs, cs], preferred_element_type=F32) + bias
            o_ref[rs, cs] = (ub[rs, cs] * mixed).astype(o_ref.dtype)


def _mixer_a(proj, w_s, b_s, ln_g, ln_b, beta, rows=256):
    t = proj.shape[0]
    heads, chunk, _ = w_s.shape
    aw = ln_g.shape[0]
    hd = aw // heads
    rows = _tile(t, rows)
    assert rows % chunk == 0
    kern = functools.partial(_amix_kernel, chunk=chunk, heads=heads, hd=hd)
    return pl.pallas_call(
        kern,
        grid=(t // rows,),
        in_specs=[pl.BlockSpec((rows, 2 * aw), lambda i: (i, 0)),
                  pl.BlockSpec((heads, chunk, chunk), lambda i: (0, 0, 0)),
                  pl.BlockSpec((chunk, heads), lambda i: (0, 0)),
                  pl.BlockSpec((1, aw), lambda i: (0, 0)),
                  pl.BlockSpec((1, aw), lambda i: (0, 0)),
                  pl.BlockSpec((1, aw), lambda i: (0, 0))],
        out_specs=pl.BlockSpec((rows, aw), lambda i: (i, 0)),
        out_shape=jax.ShapeDtypeStruct((t, aw), BF16),
        compiler_params=_params("arbitrary"),
        name="mixer_a",
    )(proj, w_s, b_s.T, ln_g.reshape(1, aw), ln_b.reshape(1, aw), beta.reshape(1, aw))


def _ret_kernel(pos_ref, invf_ref, din_ref, qd_ref, kd_ref, q_ref, k_ref, v_ref, g_ref, beta_ref,
                o_ref, state, *, heads, hd, chunk):
    @pl.when(pl.program_id(1) == 0)
    def _():
        state[...] = jnp.zeros_like(state)

    half = hd // 2
    ang = pos_ref[...] * invf_ref[...]
    cos = jnp.cos(ang)
    sin = jnp.sin(ang)
    qd_all = qd_ref[...]
    kd_all = kd_ref[...]
    k_scale = hd ** -0.5

    def rot(t):
        t1 = t[:, :half]
        t2 = t[:, half:]
        return jnp.concatenate([t1 * cos - t2 * sin, t2 * cos + t1 * sin], axis=-1)

    for h in range(heads):
        cs = slice(h * hd, (h + 1) * hd)
        chunk_decay = math.exp(math.log(1.0 - 2.0 ** (-5.0 - h)) * chunk)
        qr = rot(q_ref[:, cs].astype(F32))
        kr = rot(k_ref[:, cs].astype(F32)) * k_scale
        vb = v_ref[:, cs]
        scores = lax.dot_general(qr.astype(BF16), kr.astype(BF16), (((1,), (1,)), ((), ())),
                                 preferred_element_type=F32) * din_ref[h]
        inner = jnp.dot(scores.astype(BF16), vb, preferred_element_type=F32)
        st = state[h]
        cross = jnp.dot((qr * qd_all[:, h:h + 1]).astype(BF16), st.astype(BF16),
                        preferred_element_type=F32)
        kv = lax.dot_general((kr * kd_all[:, h:h + 1]).astype(BF16), vb, (((0,), (0,)), ((), ())),
                             preferred_element_type=F32)
        state[h] = st * chunk_decay + kv
        out = inner + cross
        out = out * lax.rsqrt(jnp.mean(out * out, axis=-1, keepdims=True) + RET_NORM_EPS)
        g = g_ref[:, cs].astype(F32)
        out = out * (g * jax.nn.sigmoid(g)) * beta_ref[:, cs]
        o_ref[:, cs] = out.astype(o_ref.dtype)


def _mixer_b(proj, positions, beta, chunk, heads):
    t = proj.shape[0]
    bsz, s = positions.shape
    bw = beta.shape[0]
    hd = bw // heads
    half = hd // 2
    n_chunks = s // chunk
    assert proj.shape[1] == 6 * bw
    log_gamma = jnp.log(1.0 - 2.0 ** (-5.0 - jnp.arange(heads, dtype=F32)))
    idx = jnp.arange(chunk, dtype=F32)
    diff = idx[:, None] - idx[None, :]
    decay_in = jnp.where(diff >= 0, jnp.exp(log_gamma[:, None, None] * jnp.maximum(diff, 0.0)), 0.0)
    q_decay = jnp.exp(log_gamma[None, :] * (idx[:, None] + 1.0))
    k_decay = jnp.exp(log_gamma[None, :] * (chunk - 1.0 - idx[:, None]))
    inv_freq = (ROPE_BASE ** (-jnp.arange(half, dtype=F32) / half)).reshape(1, half)
    pos = positions.astype(F32).reshape(t, 1)

    kern = functools.partial(_ret_kernel, heads=heads, hd=hd, chunk=chunk)
    row_map = lambda b, c: (b * n_chunks + c, 0)
    const2 = lambda b, c: (0, 0)

    def col_block(jb):
        return pl.BlockSpec((chunk, bw), lambda b, c: (b * n_chunks + c, jb))

    return pl.pallas_call(
        kern,
        grid=(bsz, n_chunks),
        in_specs=[pl.BlockSpec((chunk, 1), row_map),
                  pl.BlockSpec((1, half), const2),
                  pl.BlockSpec((heads, chunk, chunk), lambda b, c: (0, 0, 0)),
                  pl.BlockSpec((chunk, heads), const2),
                  pl.BlockSpec((chunk, heads), const2),
                  col_block(2), col_block(3), col_block(4), col_block(5),
                  pl.BlockSpec((1, bw), const2)],
        out_specs=pl.BlockSpec((chunk, bw), row_map),
        out_shape=jax.ShapeDtypeStruct((t, bw), BF16),
        scratch_shapes=[pltpu.VMEM((heads, hd, hd), F32)],
        compiler_params=_params("arbitrary", "arbitrary"),
        name="mixer_b",
    )(pos, inv_freq, decay_in, q_decay, k_decay, proj, proj, proj, proj, beta.reshape(1, bw))


def _xattn_kernel(q_ref, k_ref, v_ref, o_ref, *, heads, scale):
    dh = q_ref.shape[1] // heads
    for h in range(heads):
        cs = slice(h * dh, (h + 1) * dh)
        s = lax.dot_general(q_ref[:, cs], k_ref[:, cs], (((1,), (1,)), ((), ())),
                            preferred_element_type=F32) * scale
        m = jnp.max(s, axis=-1, keepdims=True)
        p = jnp.exp(s - m)
        l = jnp.sum(p, axis=-1, keepdims=True)
        o = jnp.dot(p.astype(BF16), v_ref[:, cs], preferred_element_type=F32)
        o_ref[:, cs] = (o / l).astype(o_ref.dtype)


def _xattn(q, kv, bsz, heads, tq=512):
    t, d = q.shape
    s = t // bsz
    m = kv.shape[0] // bsz
    dh = d // heads
    tq = _tile(s, tq)
    nq = s // tq
    return pl.pallas_call(
        functools.partial(_xattn_kernel, heads=heads, scale=dh ** -0.5),
        grid=(bsz, nq),
        in_specs=[pl.BlockSpec((tq, d), lambda b, i: (b * nq + i, 0)),
                  pl.BlockSpec((m, d), lambda b, i: (b, 0)),
                  pl.BlockSpec((m, d), lambda b, i: (b, 1))],
        out_specs=pl.BlockSpec((tq, d), lambda b, i: (b * nq + i, 0)),
        out_shape=jax.ShapeDtypeStruct((t, d), BF16),
        compiler_params=_params("arbitrary", "arbitrary"),
        name="xattn",
    )(q, kv, kv)


def _router_kernel(x_ref, g_ref, wr_ref, br_ref, h_ref, idx_ref, w_ref, rank_ref, cnt_ref, cnt, *,
                   n_exp, top_k):
    @pl.when(pl.program_id(0) == 0)
    def _():
        cnt[...] = jnp.zeros_like(cnt)

    x = x_ref[...]
    ms = jnp.mean(x * x, axis=-1, keepdims=True)
    h = x * lax.rsqrt(ms + NORM_EPS) * g_ref[...]
    _store_row_slabs(h_ref, 0, _pack_pairs(h))
    wr = wr_ref[...]
    h_hi = h.astype(BF16)
    h_lo = (h - h_hi.astype(F32)).astype(BF16)
    w_hi = wr.astype(BF16)
    w_lo = (wr - w_hi.astype(F32)).astype(BF16)
    logits = (jnp.dot(h_hi, w_hi, preferred_element_type=F32)
              + jnp.dot(h_hi, w_lo, preferred_element_type=F32)
              + jnp.dot(h_lo, w_hi, preferred_element_type=F32)) + br_ref[...]
    tt = x.shape[0]
    lane = lax.broadcasted_iota(jnp.int32, (tt, n_exp), 1)
    work = logits
    vals, idxs = [], []
    onehot = jnp.zeros((tt, n_exp), F32)
    for _ in range(top_k):
        mx = jnp.max(work, axis=-1, keepdims=True)
        ix = jnp.min(jnp.where(work == mx, lane, n_exp), axis=-1, keepdims=True)
        sel = lane == ix
        onehot = jnp.where(sel, 1.0, onehot)
        work = jnp.where(sel, -jnp.inf, work)
        vals.append(mx)
        idxs.append(ix)
    exps = [jnp.exp(v - vals[0]) for v in vals]
    denom = exps[0]
    for e in exps[1:]:
        denom = denom + e
    r = lax.broadcasted_iota(jnp.int32, (tt, tt), 0)
    c = lax.broadcasted_iota(jnp.int32, (tt, tt), 1)
    lower = jnp.where(r > c, 1.0, 0.0).astype(BF16)
    before = jnp.dot(lower, onehot.astype(BF16), preferred_element_type=F32) + cnt[...]
    for k in range(top_k):
        idx_ref[:, k:k + 1] = idxs[k]
        w_ref[:, k:k + 1] = exps[k] / denom
        rk = jnp.sum(jnp.where(lane == idxs[k], before, 0.0), axis=-1, keepdims=True)
        rank_ref[:, k:k + 1] = rk.astype(jnp.int32)
    cnt[...] = cnt[...] + jnp.sum(onehot, axis=0, keepdims=True)
    cnt_ref[...] = cnt[...]


def _router(x, g, w_router, b_router, tt=256):
    t, d = x.shape
    n_exp = w_router.shape[1]
    tt = _tile(t, tt)
    kern = functools.partial(_router_kernel, n_exp=n_exp, top_k=TOP_K)
    tok_map = lambda i: (i, 0)
    const = lambda i: (0, 0)
    return pl.pallas_call(
        kern,
        grid=(t // tt,),
        in_specs=[pl.BlockSpec((tt, d), tok_map),
                  pl.BlockSpec((1, d), const),
                  pl.BlockSpec((d, n_exp), const),
                  pl.BlockSpec((1, n_exp), const)],
        out_specs=[pl.BlockSpec((tt * (d // 2 // LANES), LANES), tok_map),
                   pl.BlockSpec((tt, TOP_K), tok_map),
                   pl.BlockSpec((tt, TOP_K), tok_map),
                   pl.BlockSpec((tt, TOP_K), tok_map),
                   pl.BlockSpec((1, n_exp), const)],
        out_shape=[jax.ShapeDtypeStruct((t * (d // 2 // LANES), LANES), jnp.uint32),
                   jax.ShapeDtypeStruct((t, TOP_K), jnp.int32),
                   jax.ShapeDtypeStruct((t, TOP_K), F32),
                   jax.ShapeDtypeStruct((t, TOP_K), jnp.int32),
                   jax.ShapeDtypeStruct((1, n_exp), F32)],
        scratch_shapes=[pltpu.VMEM((1, n_exp), F32)],
        compiler_params=_params("arbitrary"),
        name="router",
    )(x, g.reshape(1, d), w_router, b_router.reshape(1, n_exp))


def _column_groups(width):
    return max(1, min(4, width // (2 * LANES)))


def _run_tables(block_e, n_used):
    n = block_e.shape[0]
    i = jnp.arange(n, dtype=jnp.int32)
    used = i < n_used[0]
    prev_e = jnp.concatenate([jnp.full((1,), -1, jnp.int32), block_e[:-1]])
    first = used & ((i == 0) | (block_e != prev_e))
    slot = (jnp.cumsum(first.astype(jnp.int32)) - 1) % 2
    jrun = i - lax.cummax(jnp.where(first, i, 0))
    first_pos = jnp.where(first, i, n)
    next_first = jnp.concatenate([lax.cummin(first_pos, reverse=True)[1:], jnp.full((1,), n, jnp.int32)])
    nxt = jnp.where(next_first < n, block_e[jnp.minimum(next_first, n - 1)], -1)
    plen = jnp.concatenate([jnp.zeros((1,), jnp.int32), jrun[:-1] + 1])
    as_i32 = lambda a: a.astype(jnp.int32)
    return as_i32(first), as_i32(slot), as_i32(jrun), as_i32(nxt), as_i32(plen)


def _weight_streamer(i, be_ref, first_ref, slot_ref, jrun_ref, nxt_ref, plen_ref,
                     w_hbm, wres, stage, wsem, rows):
    nslab = w_hbm.shape[1] // rows
    e = be_ref[i]
    slot = slot_ref[i]
    j = jrun_ref[i]
    nxt = nxt_ref[i]
    sem = wsem.at[0]

    def fetch(expert, c):
        sub = rows // SLAB_COPIES
        return [pltpu.make_async_copy(w_hbm.at[expert, pl.ds(c * rows + p * sub, sub), :],
                                      stage.at[pl.ds(p * sub, sub), :], sem)
                for p in range(SLAB_COPIES)]

    def cast_into(dst, c):
        wres[dst, pl.ds(pl.multiple_of(c * rows, rows), rows), :] = stage[...].astype(BF16)

    def ensure():
        @pl.when(first_ref[i] == 1)
        def _():
            done = jnp.where(i == 0, 0, jnp.minimum(plen_ref[i], nslab))

            def body(c, carry):
                copies = fetch(e, c)
                for cp in copies:
                    cp.start()
                for cp in copies:
                    cp.wait()
                cast_into(slot, c)
                return carry

            lax.fori_loop(done, nslab, body, 0)

    prefetching = jnp.logical_and(j < nslab, nxt >= 0)

    def start():
        @pl.when(prefetching)
        def _():
            for cp in fetch(nxt, j):
                cp.start()

    def finish():
        @pl.when(prefetching)
        def _():
            for cp in fetch(nxt, j):
                cp.wait()
            cast_into(1 - slot, j)

    return slot, ensure, start, finish


def _whole_expert_streamer(i, be_ref, first_ref, slot_ref, jrun_ref, nxt_ref, plen_ref,
                           w_hbm, wres, stage, wsem, lag):
    e = be_ref[i]
    slot = slot_ref[i]
    j = jrun_ref[i]
    nxt = nxt_ref[i]
    sem = wsem.at[0]
    sub = stage.shape[0] // SLAB_COPIES

    def fetch(expert):
        return [pltpu.make_async_copy(w_hbm.at[expert, pl.ds(p * sub, sub), :],
                                      stage.at[pl.ds(p * sub, sub), :], sem)
                for p in range(SLAB_COPIES)]

    def land(dst):
        for cp in fetch(e):
            cp.wait()
        wres[dst] = stage[...].astype(BF16)

    def ensure():
        @pl.when(first_ref[i] == 1)
        def _():
            @pl.when(i == 0)
            def _():
                for cp in fetch(e):
                    cp.start()
                land(slot)

            @pl.when(jnp.logical_and(i > 0, plen_ref[i] <= lag))
            def _():
                land(slot)

    def start():
        @pl.when(jnp.logical_and(j == 0, nxt >= 0))
        def _():
            for cp in fetch(nxt):
                cp.start()

    def finish():
        @pl.when(jnp.logical_and(j == lag, nxt >= 0))
        def _():
            land(1 - slot)

    return slot, ensure, start, finish


def _expert_up_kernel(be_ref, nb_ref, tok_ref, first_ref, slot_ref, jrun_ref, nxt_ref, plen_ref,
                      h_hbm, w1_hbm, b1_ref, o_ref, xbuf, sem, wres, stage, wsem, *, tm, rows):
    i = pl.program_id(0)
    nb = nb_ref[0]

    rpt = xbuf.shape[1] // tm

    def row_copy(tok, r, slot):
        return pltpu.make_async_copy(h_hbm.at[pl.ds(tok * rpt, rpt), :],
                                     xbuf.at[slot, pl.ds(r * rpt, rpt), :], sem.at[slot])

    def issue(blk, slot):
        base = blk * tm

        def body(r, carry):
            row_copy(tok_ref[base + r], r, slot).start()
            return carry

        lax.fori_loop(0, tm, body, 0)

    def wait(slot):
        pltpu.make_async_copy(h_hbm.at[pl.ds(0, tm * rpt), :], xbuf.at[slot], sem.at[slot]).wait()

    @pl.when(i == 0)
    def _():
        issue(0, 0)
        issue(1, 1)

    @pl.when(i < nb)
    def _():
        wslot, ensure_weights, prefetch_start, prefetch_finish = _weight_streamer(
            i, be_ref, first_ref, slot_ref, jrun_ref, nxt_ref, plen_ref, w1_hbm, wres, stage, wsem, rows)
        ensure_weights()
        prefetch_start()
        cur = i % GATHER_SLOTS
        wait(cur)
        lo, hi = _unpack_pairs(_load_row_slabs(lambda rows_: xbuf[cur, rows_, :], tm, rpt))
        ahead = (i + 2) % GATHER_SLOTS
        base = (i + 2) * tm
        x_b = jnp.concatenate([lo.astype(BF16), hi.astype(BF16)], axis=1)
        f = o_ref.shape[1]
        lane = lax.broadcasted_iota(jnp.int32, (tm, LANES), 1)
        first = lane < LANES // 2
        idx_glu = (2 * lane) % LANES
        idx_lin = (2 * lane + 1) % LANES
        n_groups = _column_groups(f)
        gw = 2 * f // n_groups
        start_groups = max(1, n_groups - 1)
        for q in range(n_groups):
            share = range(q * tm // start_groups, (q + 1) * tm // start_groups) if q < start_groups else ()
            for r in share:
                row_copy(tok_ref[base + r], r, ahead).start()
            pin = lax.bitcast_convert_type(xbuf[cur, 0:8, 0:LANES], F32)
            cols = slice(q * gw, (q + 1) * gw)
            hh = jnp.dot(x_b, wres[wslot, :, cols], preferred_element_type=F32) + b1_ref[:, cols]
            acts = []
            for c in range(gw // (2 * LANES)):
                v0 = hh[:, 2 * c * LANES:(2 * c + 1) * LANES]
                v1 = hh[:, (2 * c + 1) * LANES:(2 * c + 2) * LANES]
                g = jnp.where(first, jnp.take_along_axis(v0, idx_glu, axis=1),
                              jnp.take_along_axis(v1, idx_glu, axis=1))
                li = jnp.where(first, jnp.take_along_axis(v0, idx_lin, axis=1),
                               jnp.take_along_axis(v1, idx_lin, axis=1))
                glu = jnp.minimum(g, SWIGLU_LIMIT)
                lin = jnp.clip(li, -SWIGLU_LIMIT, SWIGLU_LIMIT)
                act = glu * jax.nn.sigmoid(SWIGLU_ALPHA * glu) * (lin + 1.0)
                if c == 0:
                    never = glu[0:8] > SWIGLU_LIMIT
                    act = jnp.concatenate([jnp.where(never, pin, act[0:8]), act[8:]], axis=0)
                acts.append(act.astype(BF16))
            o_ref[:, q * (gw // 2):(q + 1) * (gw // 2)] = jnp.concatenate(acts, axis=1)
        prefetch_finish()

    @pl.when(i == nb)
    def _():
        wait(nb % GATHER_SLOTS)
        wait((nb + 1) % GATHER_SLOTS)

    @pl.when(i >= nb)
    def _():
        o_ref[...] = jnp.zeros_like(o_ref)


def _expert_down_kernel(be_ref, nb_ref, first_ref, slot_ref, jrun_ref, nxt_ref, plen_ref,
                        a_ref, w2_hbm, b2_ref, o_ref, wres, stage, wsem):
    i = pl.program_id(0)
    nb = nb_ref[0]

    @pl.when(i < nb)
    def _():
        wslot, ensure_weights, prefetch_start, prefetch_finish = _whole_expert_streamer(
            i, be_ref, first_ref, slot_ref, jrun_ref, nxt_ref, plen_ref, w2_hbm, wres, stage, wsem,
            W2_LAND_LAG)
        ensure_weights()
        prefetch_start()
        a = a_ref[...]
        half = wres.shape[2] // 2
        gw = half // _column_groups(half)
        for q in range(half // gw):
            cl = slice(q * gw, (q + 1) * gw)
            ch = slice(half + q * gw, half + (q + 1) * gw)
            y_lo = jnp.dot(a, wres[wslot, :, cl], preferred_element_type=F32) + b2_ref[:, cl]
            y_hi = jnp.dot(a, wres[wslot, :, ch], preferred_element_type=F32) + b2_ref[:, ch]
            o_ref[:, cl] = _pack_pairs(jnp.concatenate([y_lo, y_hi], axis=1))
        prefetch_finish()

    @pl.when(i >= nb)
    def _():
        o_ref[...] = jnp.zeros_like(o_ref)


def _experts(hp, block_e, n_used, row_tok, w1, b1, w2, b2, tm):
    n_exp, f, d = w2.shape
    half = d // 2
    rpt = half // LANES
    t = hp.shape[0] // rpt
    assert f % (2 * LANES) == 0 and hp.shape == (t * rpt, LANES)
    n_blocks = row_tok.shape[0] // tm - 1
    n_rows = n_blocks * tm
    assert (t * TOP_K) % tm == 0 and n_blocks == (t * TOP_K) // tm + n_exp
    tables = _run_tables(block_e, n_used)
    rows1 = _tile(d, W1_SLAB_ROWS)

    def emap(i, be, *_):
        return (be[i], 0, 0)

    def rmap(i, *_):
        return (i, 0)

    up_spec = pltpu.PrefetchScalarGridSpec(
        num_scalar_prefetch=8,
        grid=(n_blocks,),
        in_specs=[pl.BlockSpec(memory_space=pl.ANY),
                  pl.BlockSpec(memory_space=pl.ANY),
                  pl.BlockSpec((None, 1, 2 * f), emap)],
        out_specs=pl.BlockSpec((tm, f), rmap),
        scratch_shapes=[pltpu.VMEM((GATHER_SLOTS, tm * rpt, LANES), jnp.uint32),
                        pltpu.SemaphoreType.DMA((GATHER_SLOTS,)),
                        pltpu.VMEM((2, d, 2 * f), BF16),
                        pltpu.VMEM((rows1, 2 * f), F32),
                        pltpu.SemaphoreType.DMA((1,))],
    )
    act = pl.pallas_call(
        functools.partial(_expert_up_kernel, tm=tm, rows=rows1),
        grid_spec=up_spec,
        out_shape=jax.ShapeDtypeStruct((n_rows, f), BF16),
        compiler_params=_params("arbitrary"),
        name="experts_up",
    )(block_e, n_used, row_tok, *tables, hp, w1, b1)

    down_spec = pltpu.PrefetchScalarGridSpec(
        num_scalar_prefetch=7,
        grid=(n_blocks,),
        in_specs=[pl.BlockSpec((tm, f), rmap),
                  pl.BlockSpec(memory_space=pl.ANY),
                  pl.BlockSpec((None, 1, d), emap)],
        out_specs=pl.BlockSpec((tm, half), rmap),
        scratch_shapes=[pltpu.VMEM((2, f, d), BF16),
                        pltpu.VMEM((f, d), F32),
                        pltpu.SemaphoreType.DMA((1,))],
    )
    return pl.pallas_call(
        _expert_down_kernel,
        grid_spec=down_spec,
        out_shape=jax.ShapeDtypeStruct((n_rows, half), jnp.uint32),
        compiler_params=_params("arbitrary"),
        name="experts_down",
    )(block_e, n_used, *tables, act, w2, b2)


def _combine_kernel(pos_ref, x_ref, w_ref, g_ref, y_hbm, o_ref, ybuf, sem, *, tc, top_k, normalize):
    i = pl.program_id(0)
    n = pl.num_programs(0)

    def row_copy(p, r, k, slot):
        return pltpu.make_async_copy(y_hbm.at[pl.ds(p, 1), :], ybuf.at[slot, k, pl.ds(r, 1), :],
                                     sem.at[slot])

    def issue(blk, slot):
        base = blk * (tc * top_k)

        def body(r, carry):
            for k in range(top_k):
                row_copy(pos_ref[base + r * top_k + k], r, k, slot).start()
            return carry

        lax.fori_loop(0, tc, body, 0)

    def wait(slot):
        for k in range(top_k):
            pltpu.make_async_copy(y_hbm.at[pl.ds(0, tc), :], ybuf.at[slot, k], sem.at[slot]).wait()

    @pl.when(i == 0)
    def _():
        issue(0, 0)
        issue(1, 1)

    slot = i % GATHER_SLOTS
    wait(slot)
    w = w_ref[...]
    x = x_ref[...]
    half = x.shape[1] // 2
    acc_lo = x[:, :half]
    acc_hi = x[:, half:]
    for k in range(top_k):
        lo, hi = _unpack_pairs(ybuf[slot, k])
        acc_lo = acc_lo + lo * w[:, k:k + 1]
        acc_hi = acc_hi + hi * w[:, k:k + 1]
    acc = jnp.concatenate([acc_lo, acc_hi], axis=1)
    if normalize:
        ms = jnp.mean(acc * acc, axis=-1, keepdims=True)
        acc = acc * lax.rsqrt(ms + NORM_EPS) * g_ref[...]
    o_ref[...] = acc
    ahead = (i + 2) % GATHER_SLOTS
    base = (i + 2) * (tc * top_k)
    for r in range(tc):
        for k in range(top_k):
            row_copy(pos_ref[base + r * top_k + k], r, k, ahead).start()

    @pl.when(i == n - 1)
    def _():
        wait((n % GATHER_SLOTS))
        wait(((n + 1) % GATHER_SLOTS))


def _combine(x, top_w, pos, y_rows, g, normalize, tc=128):
    t, d = x.shape
    tc = _tile(t, tc)
    assert t // tc >= 2
    pos = jnp.concatenate([pos.reshape(-1), jnp.zeros((2 * tc * TOP_K,), jnp.int32)])
    grid_spec = pltpu.PrefetchScalarGridSpec(
        num_scalar_prefetch=1,
        grid=(t // tc,),
        in_specs=[pl.BlockSpec((tc, d), lambda i, p: (i, 0)),
                  pl.BlockSpec((tc, TOP_K), lambda i, p: (i, 0)),
                  pl.BlockSpec((1, d), lambda i, p: (0, 0)),
                  pl.BlockSpec(memory_space=pl.ANY)],
        out_specs=pl.BlockSpec((tc, d), lambda i, p: (i, 0)),
        scratch_shapes=[pltpu.VMEM((GATHER_SLOTS, TOP_K, tc, d // 2), jnp.uint32),
                        pltpu.SemaphoreType.DMA((GATHER_SLOTS,))],
    )
    return pl.pallas_call(
        functools.partial(_combine_kernel, tc=tc, top_k=TOP_K, normalize=normalize),
        grid_spec=grid_spec,
        out_shape=jax.ShapeDtypeStruct((t, d), F32),
        compiler_params=_params("arbitrary"),
        name="combine",
    )(pos, x, top_w, g.reshape(1, d), y_rows)


def _moe_block_rows(n_assign):
    return 256 if n_assign >= 256 * 32 else 128


def kernel(x, mem, positions, norm_mix_g, w_in, w_s, b_s, ln_v_g, ln_v_b, beta_a, beta_b, w_out,
           norm_x_g, norm_mem_g, w_xq, w_xkv, w_xo, norm_ffn_g, w_router, b_router, w1, b1, w2, b2,
           norm_final_g):
    bsz, s, d = x.shape
    t = bsz * s
    depth = w_in.shape[0]
    chunk = w_s.shape[2]
    n_exp = w_router.shape[2]
    xs = x.reshape(t, d)
    mem2 = mem.reshape(-1, d)

    for l in range(depth):
        h = _rmsnorm(xs, norm_mix_g[l], BF16)
        proj = _matmul([h], w_in[l], BF16, name="w_in")
        out_a = _mixer_a(proj, w_s[l], b_s[l], ln_v_g[l], ln_v_b[l], beta_a[l])
        out_b = _mixer_b(proj, positions, beta_b[l], chunk, B_HEADS)
        xs = _matmul([out_a, out_b], w_out[l], F32, res=xs, name="w_out")

        h = _rmsnorm(xs, norm_x_g[l], BF16)
        mem_n = _rmsnorm(mem2, norm_mem_g[l], BF16)
        q = _matmul([h], w_xq[l], BF16, name="w_xq")
        kv = _matmul([mem_n], w_xkv[l], BF16, name="w_xkv")
        o = _xattn(q, kv, bsz, X_HEADS)
        xs = _matmul([o], w_xo[l], F32, res=xs, name="w_xo")

        hf, top_idx, top_w, rank, counts = _router(xs, norm_ffn_g[l], w_router[l], b_router[l])
        n_assign = t * TOP_K
        tm = _moe_block_rows(n_assign)
        counts = counts.reshape(n_exp).astype(jnp.int32)
        padded = (counts + tm - 1) // tm * tm
        pends = jnp.cumsum(padded)
        pstarts = pends - padded
        pos = pstarts[top_idx] + rank
        n_blocks = -(-n_assign // tm) + n_exp
        n_rows = n_blocks * tm
        row_tok = jnp.zeros((n_rows + tm,), jnp.int32).at[pos.reshape(-1)].set(
            jnp.arange(n_assign, dtype=jnp.int32) // TOP_K)
        block_start = jnp.arange(n_blocks, dtype=jnp.int32) * tm
        block_e = jnp.sum((pends[None, :] <= block_start[:, None]).astype(jnp.int32), axis=1)
        block_e = jnp.minimum(block_e, n_exp - 1)
        n_used = (pends[-1:] // tm).astype(jnp.int32)
        f = w2.shape[2]
        y_rows = _experts(hf, block_e, n_used, row_tok, w1[l], b1[l].reshape(n_exp, 1, 2 * f),
                          w2[l], b2[l].reshape(n_exp, 1, d), tm)
        xs = _combine(xs, top_w, pos, y_rows, norm_final_g, normalize=(l + 1 == depth))
    return xs.reshape(bsz, s, d)
```

```python
import functools
import math

import jax
import jax.numpy as jnp
from jax import lax
from jax.experimental import pallas as pl
from jax.experimental.pallas import tpu as pltpu

B_HEADS = 8
X_HEADS = 4
TOP_K = 4
ROPE_BASE = 10000.0
NORM_EPS = 1e-5
RET_NORM_EPS = 1e-6
SWIGLU_LIMIT = 7.0
SWIGLU_ALPHA = 1.702

LANES = 128
GATHER_SLOTS = 3
WEIGHT_DMA_PRIORITY = 1
W1_SLAB_ROWS = 512
W2_LAND_LAG = 3
SLAB_COPIES = 4

V7X_VMEM_BYTES = 64 * 1024 * 1024
VMEM_LIMIT_BYTES = V7X_VMEM_BYTES - 6 * 1024 * 1024

F32 = jnp.float32
BF16 = jnp.bfloat16


def _params(*semantics):
    return pltpu.CompilerParams(
        dimension_semantics=semantics,
        vmem_limit_bytes=VMEM_LIMIT_BYTES,
        disable_bounds_checks=True,
    )


def _tile(n, want):
    t = min(n, want)
    while n % t:
        t //= 2
    return t


def _pack_pairs(v):
    half = v.shape[1] // 2
    bits = lax.bitcast_convert_type(v.astype(BF16).astype(F32), jnp.uint32)
    return (bits[:, :half] >> 16) | (bits[:, half:] & jnp.uint32(0xFFFF0000))


def _store_row_slabs(ref, first_chunk, words):
    m, w = words.shape
    r = ref.shape[0] // m
    for s in range(w // LANES):
        ref[pl.ds(first_chunk + s, m, stride=r), :] = words[:, s * LANES:(s + 1) * LANES]


def _load_row_slabs(read_rows, m, r):
    return jnp.concatenate([read_rows(pl.ds(s, m, stride=r)) for s in range(r)], axis=1)


def _unpack_pairs(p):
    lo = lax.bitcast_convert_type(p << 16, F32)
    hi = lax.bitcast_convert_type(p & jnp.uint32(0xFFFF0000), F32)
    return lo, hi


def _rms_kernel(x_ref, g_ref, o_ref):
    x = x_ref[...]
    ms = jnp.mean(x * x, axis=-1, keepdims=True)
    o_ref[...] = (x * lax.rsqrt(ms + NORM_EPS) * g_ref[...]).astype(o_ref.dtype)


def _rmsnorm(x, g, out_dtype, tm=512):
    m, d = x.shape
    tm = _tile(m, tm)
    return pl.pallas_call(
        _rms_kernel,
        grid=(m // tm,),
        in_specs=[pl.BlockSpec((tm, d), lambda i: (i, 0)),
                  pl.BlockSpec((1, d), lambda i: (0, 0))],
        out_specs=pl.BlockSpec((tm, d), lambda i: (i, 0)),
        out_shape=jax.ShapeDtypeStruct((m, d), out_dtype),
        compiler_params=_params("arbitrary"),
        name="rmsnorm",
    )(x, g.reshape(1, d))


def _mm_kernel(*refs, n_a, has_res):
    a_refs = refs[:n_a]
    w_ref = refs[n_a]
    r_ref = refs[n_a + 1] if has_res else None
    o_ref, wbf = refs[-2:]

    @pl.when(pl.program_id(1) == 0)
    def _():
        wbf[...] = w_ref[...].astype(BF16)

    acc = None
    k0 = 0
    for a_ref in a_refs:
        ka = a_ref.shape[1]
        part = jnp.dot(a_ref[...], wbf[k0:k0 + ka, :], preferred_element_type=F32)
        acc = part if acc is None else acc + part
        k0 += ka
    if has_res:
        acc = acc + r_ref[...]
    o_ref[...] = acc.astype(o_ref.dtype)


def _matmul(a_parts, w, out_dtype, res=None, tm=1024, tn=512, name="matmul"):
    m = a_parts[0].shape[0]
    k, n = w.shape
    assert sum(a.shape[1] for a in a_parts) == k
    tm = _tile(m, tm)
    tn = _tile(n, tn)
    in_specs = [pl.BlockSpec((tm, a.shape[1]), lambda j, i: (i, 0)) for a in a_parts]
    in_specs.append(pl.BlockSpec((k, tn), lambda j, i: (0, j)))
    args = list(a_parts) + [w]
    if res is not None:
        in_specs.append(pl.BlockSpec((tm, tn), lambda j, i: (i, j)))
        args.append(res)
    return pl.pallas_call(
        functools.partial(_mm_kernel, n_a=len(a_parts), has_res=res is not None),
        grid=(n // tn, m // tm),
        in_specs=in_specs,
        out_specs=pl.BlockSpec((tm, tn), lambda j, i: (i, j)),
        out_shape=jax.ShapeDtypeStruct((m, n), out_dtype),
        scratch_shapes=[pltpu.VMEM((k, tn), BF16)],
        compiler_params=_params("arbitrary", "arbitrary"),
        name=name,
    )(*args)


def _amix_kernel(a_ref, ws_ref, bst_ref, lng_ref, lnb_ref, beta_ref, o_ref, *, chunk, heads, hd):
    aw = heads * hd
    a = a_ref[...].astype(F32)
    hid = 0.5 * a * (1.0 + lax.erf(a * (2.0 ** -0.5)))
    u = hid[:, :aw]
    v = hid[:, aw:]
    mean = jnp.mean(v, axis=-1, keepdims=True)
    vc = v - mean
    var = jnp.mean(vc * vc, axis=-1, keepdims=True)
    vn = (vc * lax.rsqrt(var + NORM_EPS) * lng_ref[...] + lnb_ref[...]).astype(BF16)
    ub = u * beta_ref[...]
    rows = a.shape[0]
    row = lax.broadcasted_iota(jnp.int32, (chunk, chunk), 0)
    col = lax.broadcasted_iota(jnp.int32, (chunk, chunk), 1)
    causal = row >= col
    bst = bst_ref[...]
    for h in range(heads):
        w = jnp.where(causal, ws_ref[h], 0.0).astype(BF16)
        bias = bst[:, h:h + 1]
        for c in range(rows // chunk):
            rs = slice(c * chunk, (c + 1) * chunk)
            cs = slice(h * hd, (h + 1) * hd)
            mixed = jnp.dot(w, vn[rs, cs], preferred_element_type=F32) + bias
            o_ref[rs, cs] = (ub[rs, cs] * mixed).astype(o_ref.dtype)


def _mixer_a(proj, w_s, b_s, ln_g, ln_b, beta, rows=256):
    t = proj.shape[0]
    heads, chunk, _ = w_s.shape
    aw = ln_g.shape[0]
    hd = aw // heads
    rows = _tile(t, rows)
    assert rows % chunk == 0
    kern = functools.partial(_amix_kernel, chunk=chunk, heads=heads, hd=hd)
    return pl.pallas_call(
        kern,
        grid=(t // rows,),
        in_specs=[pl.BlockSpec((rows, 2 * aw), lambda i: (i, 0)),
                  pl.BlockSpec((heads, chunk, chunk), lambda i: (0, 0, 0)),
                  pl.BlockSpec((chunk, heads), lambda i: (0, 0)),
                  pl.BlockSpec((1, aw), lambda i: (0, 0)),
                  pl.BlockSpec((1, aw), lambda i: (0, 0)),
                  pl.BlockSpec((1, aw), lambda i: (0, 0))],
        out_specs=pl.BlockSpec((rows, aw), lambda i: (i, 0)),
        out_shape=jax.ShapeDtypeStruct((t, aw), BF16),
        compiler_params=_params("arbitrary"),
        name="mixer_a",
    )(proj, w_s, b_s.T, ln_g.reshape(1, aw), ln_b.reshape(1, aw), beta.reshape(1, aw))


def _ret_kernel(pos_ref, invf_ref, din_ref, qd_ref, kd_ref, q_ref, k_ref, v_ref, g_ref, beta_ref,
                o_ref, state, *, heads, hd, chunk):
    @pl.when(pl.program_id(1) == 0)
    def _():
        state[...] = jnp.zeros_like(state)

    half = hd // 2
    ang = pos_ref[...] * invf_ref[...]
    cos = jnp.cos(ang)
    sin = jnp.sin(ang)
    qd_all = qd_ref[...]
    kd_all = kd_ref[...]
    k_scale = hd ** -0.5

    def rot(t):
        t1 = t[:, :half]
        t2 = t[:, half:]
        return jnp.concatenate([t1 * cos - t2 * sin, t2 * cos + t1 * sin], axis=-1)

    for h in range(heads):
        cs = slice(h * hd, (h + 1) * hd)
        chunk_decay = math.exp(math.log(1.0 - 2.0 ** (-5.0 - h)) * chunk)
        qr = rot(q_ref[:, cs].astype(F32))
        kr = rot(k_ref[:, cs].astype(F32)) * k_scale
        vb = v_ref[:, cs]
        scores = lax.dot_general(qr.astype(BF16), kr.astype(BF16), (((1,), (1,)), ((), ())),
                                 preferred_element_type=F32) * din_ref[h]
        inner = jnp.dot(scores.astype(BF16), vb, preferred_element_type=F32)
        st = state[h]
        cross = jnp.dot((qr * qd_all[:, h:h + 1]).astype(BF16), st.astype(BF16),
                        preferred_element_type=F32)
        kv = lax.dot_general((kr * kd_all[:, h:h + 1]).astype(BF16), vb, (((0,), (0,)), ((), ())),
                             preferred_element_type=F32)
        state[h] = st * chunk_decay + kv
        out = inner + cross
        out = out * lax.rsqrt(jnp.mean(out * out, axis=-1, keepdims=True) + RET_NORM_EPS)
        g = g_ref[:, cs].astype(F32)
        out = out * (g * jax.nn.sigmoid(g)) * beta_ref[:, cs]
        o_ref[:, cs] = out.astype(o_ref.dtype)


def _mixer_b(proj, positions, beta, chunk, heads):
    t = proj.shape[0]
    bsz, s = positions.shape
    bw = beta.shape[0]
    hd = bw // heads
    half = hd // 2
    n_chunks = s // chunk
    assert proj.shape[1] == 6 * bw
    log_gamma = jnp.log(1.0 - 2.0 ** (-5.0 - jnp.arange(heads, dtype=F32)))
    idx = jnp.arange(chunk, dtype=F32)
    diff = idx[:, None] - idx[None, :]
    decay_in = jnp.where(diff >= 0, jnp.exp(log_gamma[:, None, None] * jnp.maximum(diff, 0.0)), 0.0)
    q_decay = jnp.exp(log_gamma[None, :] * (idx[:, None] + 1.0))
    k_decay = jnp.exp(log_gamma[None, :] * (chunk - 1.0 - idx[:, None]))
    inv_freq = (ROPE_BASE ** (-jnp.arange(half, dtype=F32) / half)).reshape(1, half)
    pos = positions.astype(F32).reshape(t, 1)

    kern = functools.partial(_ret_kernel, heads=heads, hd=hd, chunk=chunk)
    row_map = lambda b, c: (b * n_chunks + c, 0)
    const2 = lambda b, c: (0, 0)

    def col_block(jb):
        return pl.BlockSpec((chunk, bw), lambda b, c: (b * n_chunks + c, jb))

    return pl.pallas_call(
        kern,
        grid=(bsz, n_chunks),
        in_specs=[pl.BlockSpec((chunk, 1), row_map),
                  pl.BlockSpec((1, half), const2),
                  pl.BlockSpec((heads, chunk, chunk), lambda b, c: (0, 0, 0)),
                  pl.BlockSpec((chunk, heads), const2),
                  pl.BlockSpec((chunk, heads), const2),
                  col_block(2), col_block(3), col_block(4), col_block(5),
                  pl.BlockSpec((1, bw), const2)],
        out_specs=pl.BlockSpec((chunk, bw), row_map),
        out_shape=jax.ShapeDtypeStruct((t, bw), BF16),
        scratch_shapes=[pltpu.VMEM((heads, hd, hd), F32)],
        compiler_params=_params("arbitrary", "arbitrary"),
        name="mixer_b",
    )(pos, inv_freq, decay_in, q_decay, k_decay, proj, proj, proj, proj, beta.reshape(1, bw))


def _xattn_kernel(q_ref, k_ref, v_ref, o_ref, *, heads, scale):
    dh = q_ref.shape[1] // heads
    for h in range(heads):
        cs = slice(h * dh, (h + 1) * dh)
        s = lax.dot_general(q_ref[:, cs], k_ref[:, cs], (((1,), (1,)), ((), ())),
                            preferred_element_type=F32) * scale
        m = jnp.max(s, axis=-1, keepdims=True)
        p = jnp.exp(s - m)
        l = jnp.sum(p, axis=-1, keepdims=True)
        o = jnp.dot(p.astype(BF16), v_ref[:, cs], preferred_element_type=F32)
        o_ref[:, cs] = (o / l).astype(o_ref.dtype)


def _xattn(q, kv, bsz, heads, tq=512):
    t, d = q.shape
    s = t // bsz
    m = kv.shape[0] // bsz
    dh = d // heads
    tq = _tile(s, tq)
    nq = s // tq
    return pl.pallas_call(
        functools.partial(_xattn_kernel, heads=heads, scale=dh ** -0.5),
        grid=(bsz, nq),
        in_specs=[pl.BlockSpec((tq, d), lambda b, i: (b * nq + i, 0)),
                  pl.BlockSpec((m, d), lambda b, i: (b, 0)),
                  pl.BlockSpec((m, d), lambda b, i: (b, 1))],
        out_specs=pl.BlockSpec((tq, d), lambda b, i: (b * nq + i, 0)),
        out_shape=jax.ShapeDtypeStruct((t, d), BF16),
        compiler_params=_params("arbitrary", "arbitrary"),
        name="xattn",
    )(q, kv, kv)


def _router_kernel(x_ref, g_ref, wr_ref, br_ref, h_ref, idx_ref, w_ref, rank_ref, cnt_ref, cnt, *,
                   n_exp, top_k):
    @pl.when(pl.program_id(0) == 0)
    def _():
        cnt[...] = jnp.zeros_like(cnt)

    x = x_ref[...]
    ms = jnp.mean(x * x, axis=-1, keepdims=True)
    h = x * lax.rsqrt(ms + NORM_EPS) * g_ref[...]
    _store_row_slabs(h_ref, 0, _pack_pairs(h))
    wr = wr_ref[...]
    h_hi = h.astype(BF16)
    h_lo = (h - h_hi.astype(F32)).astype(BF16)
    w_hi = wr.astype(BF16)
    w_lo = (wr - w_hi.astype(F32)).astype(BF16)
    logits = (jnp.dot(h_hi, w_hi, preferred_element_type=F32)
              + jnp.dot(h_hi, w_lo, preferred_element_type=F32)
              + jnp.dot(h_lo, w_hi, preferred_element_type=F32)) + br_ref[...]
    tt = x.shape[0]
    lane = lax.broadcasted_iota(jnp.int32, (tt, n_exp), 1)
    work = logits
    vals, idxs = [], []
    onehot = jnp.zeros((tt, n_exp), F32)
    for _ in range(top_k):
        mx = jnp.max(work, axis=-1, keepdims=True)
        ix = jnp.min(jnp.where(work == mx, lane, n_exp), axis=-1, keepdims=True)
        sel = lane == ix
        onehot = jnp.where(sel, 1.0, onehot)
        work = jnp.where(sel, -jnp.inf, work)
        vals.append(mx)
        idxs.append(ix)
    exps = [jnp.exp(v - vals[0]) for v in vals]
    denom = exps[0]
    for e in exps[1:]:
        denom = denom + e
    r = lax.broadcasted_iota(jnp.int32, (tt, tt), 0)
    c = lax.broadcasted_iota(jnp.int32, (tt, tt), 1)
    lower = jnp.where(r > c, 1.0, 0.0).astype(BF16)
    before = jnp.dot(lower, onehot.astype(BF16), preferred_element_type=F32) + cnt[...]
    for k in range(top_k):
        idx_ref[:, k:k + 1] = idxs[k]
        w_ref[:, k:k + 1] = exps[k] / denom
        rk = jnp.sum(jnp.where(lane == idxs[k], before, 0.0), axis=-1, keepdims=True)
        rank_ref[:, k:k + 1] = rk.astype(jnp.int32)
    cnt[...] = cnt[...] + jnp.sum(onehot, axis=0, keepdims=True)
    cnt_ref[...] = cnt[...]


def _router(x, g, w_router, b_router, tt=256):
    t, d = x.shape
    n_exp = w_router.shape[1]
    tt = _tile(t, tt)
    kern = functools.partial(_router_kernel, n_exp=n_exp, top_k=TOP_K)
    tok_map = lambda i: (i, 0)
    const = lambda i: (0, 0)
    return pl.pallas_call(
        kern,
        grid=(t // tt,),
        in_specs=[pl.BlockSpec((tt, d), tok_map),
                  pl.BlockSpec((1, d), const),
                  pl.BlockSpec((d, n_exp), const),
                  pl.BlockSpec((1, n_exp), const)],
        out_specs=[pl.BlockSpec((tt * (d // 2 // LANES), LANES), tok_map),
                   pl.BlockSpec((tt, TOP_K), tok_map),
                   pl.BlockSpec((tt, TOP_K), tok_map),
                   pl.BlockSpec((tt, TOP_K), tok_map),
                   pl.BlockSpec((1, n_exp), const)],
        out_shape=[jax.ShapeDtypeStruct((t * (d // 2 // LANES), LANES), jnp.uint32),
                   jax.ShapeDtypeStruct((t, TOP_K), jnp.int32),
                   jax.ShapeDtypeStruct((t, TOP_K), F32),
                   jax.ShapeDtypeStruct((t, TOP_K), jnp.int32),
                   jax.ShapeDtypeStruct((1, n_exp), F32)],
        scratch_shapes=[pltpu.VMEM((1, n_exp), F32)],
        compiler_params=_params("arbitrary"),
        name="router",
    )(x, g.reshape(1, d), w_router, b_router.reshape(1, n_exp))


def _column_groups(width):
    return max(1, min(4, width // (2 * LANES)))


def _run_tables(block_e, n_used):
    n = block_e.shape[0]
    i = jnp.arange(n, dtype=jnp.int32)
    used = i < n_used[0]
    prev_e = jnp.concatenate([jnp.full((1,), -1, jnp.int32), block_e[:-1]])
    first = used & ((i == 0) | (block_e != prev_e))
    slot = (jnp.cumsum(first.astype(jnp.int32)) - 1) % 2
    jrun = i - lax.cummax(jnp.where(first, i, 0))
    first_pos = jnp.where(first, i, n)
    next_first = jnp.concatenate([lax.cummin(first_pos, reverse=True)[1:], jnp.full((1,), n, jnp.int32)])
    nxt = jnp.where(next_first < n, block_e[jnp.minimum(next_first, n - 1)], -1)
    plen = jnp.concatenate([jnp.zeros((1,), jnp.int32), jrun[:-1] + 1])
    as_i32 = lambda a: a.astype(jnp.int32)
    return as_i32(first), as_i32(slot), as_i32(jrun), as_i32(nxt), as_i32(plen)


def _weight_streamer(i, be_ref, first_ref, slot_ref, jrun_ref, nxt_ref, plen_ref,
                     w_hbm, wres, stage, wsem, rows):
    nslab = w_hbm.shape[1] // rows
    e = be_ref[i]
    slot = slot_ref[i]
    j = jrun_ref[i]
    nxt = nxt_ref[i]
    sem = wsem.at[0]

    def fetch(expert, c):
        sub = rows // SLAB_COPIES
        return [pltpu.make_async_copy(w_hbm.at[expert, pl.ds(c * rows + p * sub, sub), :],
                                      stage.at[pl.ds(p * sub, sub), :], sem)
                for p in range(SLAB_COPIES)]

    def cast_into(dst, c):
        wres[dst, pl.ds(pl.multiple_of(c * rows, rows), rows), :] = stage[...].astype(BF16)

    def ensure():
        @pl.when(first_ref[i] == 1)
        def _():
            done = jnp.where(i == 0, 0, jnp.minimum(plen_ref[i], nslab))

            def body(c, carry):
                copies = fetch(e, c)
                for cp in copies:
                    cp.start(priority=WEIGHT_DMA_PRIORITY)
                for cp in copies:
                    cp.wait()
                cast_into(slot, c)
                return carry

            lax.fori_loop(done, nslab, body, 0)

    prefetching = jnp.logical_and(j < nslab, nxt >= 0)

    def start():
        @pl.when(prefetching)
        def _():
            for cp in fetch(nxt, j):
                cp.start(priority=WEIGHT_DMA_PRIORITY)

    def finish():
        @pl.when(prefetching)
        def _():
            for cp in fetch(nxt, j):
                cp.wait()
            cast_into(1 - slot, j)

    return slot, ensure, start, finish


def _whole_expert_streamer(i, be_ref, first_ref, slot_ref, jrun_ref, nxt_ref, plen_ref,
                           w_hbm, wres, stage, wsem, lag):
    e = be_ref[i]
    slot = slot_ref[i]
    j = jrun_ref[i]
    nxt = nxt_ref[i]
    sem = wsem.at[0]
    sub = stage.shape[0] // SLAB_COPIES

    def fetch(expert):
        return [pltpu.make_async_copy(w_hbm.at[expert, pl.ds(p * sub, sub), :],
                                      stage.at[pl.ds(p * sub, sub), :], sem)
                for p in range(SLAB_COPIES)]

    def land(dst):
        for cp in fetch(e):
            cp.wait()
        wres[dst] = stage[...].astype(BF16)

    def ensure():
        @pl.when(first_ref[i] == 1)
        def _():
            @pl.when(i == 0)
            def _():
                for cp in fetch(e):
                    cp.start(priority=WEIGHT_DMA_PRIORITY)
                land(slot)

            @pl.when(jnp.logical_and(i > 0, plen_ref[i] <= lag))
            def _():
                land(slot)

    def start():
        @pl.when(jnp.logical_and(j == 0, nxt >= 0))
        def _():
            for cp in fetch(nxt):
                cp.start(priority=WEIGHT_DMA_PRIORITY)

    def finish():
        @pl.when(jnp.logical_and(j == lag, nxt >= 0))
        def _():
            land(1 - slot)

    return slot, ensure, start, finish


def _expert_up_kernel(be_ref, nb_ref, tok_ref, first_ref, slot_ref, jrun_ref, nxt_ref, plen_ref,
                      h_hbm, w1_hbm, b1_ref, o_ref, xbuf, sem, wres, stage, wsem, *, tm, rows):
    i = pl.program_id(0)
    nb = nb_ref[0]

    rpt = xbuf.shape[1] // tm

    def row_copy(tok, r, slot):
        return pltpu.make_async_copy(h_hbm.at[pl.ds(tok * rpt, rpt), :],
                                     xbuf.at[slot, pl.ds(r * rpt, rpt), :], sem.at[slot])

    def issue(blk, slot):
        base = blk * tm

        def body(r, carry):
            row_copy(tok_ref[base + r], r, slot).start()
            return carry

        lax.fori_loop(0, tm, body, 0)

    def wait(slot):
        pltpu.make_async_copy(h_hbm.at[pl.ds(0, tm * rpt), :], xbuf.at[slot], sem.at[slot]).wait()

    @pl.when(i == 0)
    def _():
        issue(0, 0)
        issue(1, 1)

    @pl.when(i < nb)
    def _():
        wslot, ensure_weights, prefetch_start, prefetch_finish = _weight_streamer(
            i, be_ref, first_ref, slot_ref, jrun_ref, nxt_ref, plen_ref, w1_hbm, wres, stage, wsem, rows)
        ensure_weights()
        prefetch_start()
        cur = i % GATHER_SLOTS
        wait(cur)
        lo, hi = _unpack_pairs(_load_row_slabs(lambda rows_: xbuf[cur, rows_, :], tm, rpt))
        ahead = (i + 2) % GATHER_SLOTS
        base = (i + 2) * tm
        x_b = jnp.concatenate([lo.astype(BF16), hi.astype(BF16)], axis=1)
        f = o_ref.shape[1]
        lane = lax.broadcasted_iota(jnp.int32, (tm, LANES), 1)
        first = lane < LANES // 2
        idx_glu = (2 * lane) % LANES
        idx_lin = (2 * lane + 1) % LANES
        n_groups = _column_groups(f)
        gw = 2 * f // n_groups
        start_groups = max(1, n_groups - 1)
        for q in range(n_groups):
            share = range(q * tm // start_groups, (q + 1) * tm // start_groups) if q < start_groups else ()
            for r in share:
                row_copy(tok_ref[base + r], r, ahead).start()
            pin = lax.bitcast_convert_type(xbuf[cur, 0:8, 0:LANES], F32)
            cols = slice(q * gw, (q + 1) * gw)
            hh = jnp.dot(x_b, wres[wslot, :, cols], preferred_element_type=F32) + b1_ref[:, cols]
            acts = []
            for c in range(gw // (2 * LANES)):
                v0 = hh[:, 2 * c * LANES:(2 * c + 1) * LANES]
                v1 = hh[:, (2 * c + 1) * LANES:(2 * c + 2) * LANES]
                g = jnp.where(first, jnp.take_along_axis(v0, idx_glu, axis=1),
                              jnp.take_along_axis(v1, idx_glu, axis=1))
                li = jnp.where(first, jnp.take_along_axis(v0, idx_lin, axis=1),
                               jnp.take_along_axis(v1, idx_lin, axis=1))
                glu = jnp.minimum(g, SWIGLU_LIMIT)
                lin = jnp.clip(li, -SWIGLU_LIMIT, SWIGLU_LIMIT)
                act = glu * jax.nn.sigmoid(SWIGLU_ALPHA * glu) * (lin + 1.0)
                if c == 0:
                    never = glu[0:8] > SWIGLU_LIMIT
                    act = jnp.concatenate([jnp.where(never, pin, act[0:8]), act[8:]], axis=0)
                acts.append(act.astype(BF16))
            o_ref[:, q * (gw // 2):(q + 1) * (gw // 2)] = jnp.concatenate(acts, axis=1)
        prefetch_finish()

    @pl.when(i == nb)
    def _():
        wait(nb % GATHER_SLOTS)
        wait((nb + 1) % GATHER_SLOTS)

    @pl.when(i >= nb)
    def _():
        o_ref[...] = jnp.zeros_like(o_ref)


def _expert_down_kernel(be_ref, nb_ref, first_ref, slot_ref, jrun_ref, nxt_ref, plen_ref,
                        a_ref, w2_hbm, b2_ref, o_ref, wres, stage, wsem):
    i = pl.program_id(0)
    nb = nb_ref[0]

    @pl.when(i < nb)
    def _():
        wslot, ensure_weights, prefetch_start, prefetch_finish = _whole_expert_streamer(
            i, be_ref, first_ref, slot_ref, jrun_ref, nxt_ref, plen_ref, w2_hbm, wres, stage, wsem,
            W2_LAND_LAG)
        ensure_weights()
        prefetch_start()
        a = a_ref[...]
        half = wres.shape[2] // 2
        gw = half // _column_groups(half)
        for q in range(half // gw):
            cl = slice(q * gw, (q + 1) * gw)
            ch = slice(half + q * gw, half + (q + 1) * gw)
            y_lo = jnp.dot(a, wres[wslot, :, cl], preferred_element_type=F32) + b2_ref[:, cl]
            y_hi = jnp.dot(a, wres[wslot, :, ch], preferred_element_type=F32) + b2_ref[:, ch]
            o_ref[:, cl] = _pack_pairs(jnp.concatenate([y_lo, y_hi], axis=1))
        prefetch_finish()

    @pl.when(i >= nb)
    def _():
        o_ref[...] = jnp.zeros_like(o_ref)


def _experts(hp, block_e, n_used, row_tok, w1, b1, w2, b2, tm):
    n_exp, f, d = w2.shape
    half = d // 2
    rpt = half // LANES
    t = hp.shape[0] // rpt
    assert f % (2 * LANES) == 0 and hp.shape == (t * rpt, LANES)
    n_blocks = row_tok.shape[0] // tm - 1
    n_rows = n_blocks * tm
    assert (t * TOP_K) % tm == 0 and n_blocks == (t * TOP_K) // tm + n_exp
    tables = _run_tables(block_e, n_used)
    rows1 = _tile(d, W1_SLAB_ROWS)

    def emap(i, be, *_):
        return (be[i], 0, 0)

    def rmap(i, *_):
        return (i, 0)

    up_spec = pltpu.PrefetchScalarGridSpec(
        num_scalar_prefetch=8,
        grid=(n_blocks,),
        in_specs=[pl.BlockSpec(memory_space=pl.ANY),
                  pl.BlockSpec(memory_space=pl.ANY),
                  pl.BlockSpec((None, 1, 2 * f), emap)],
        out_specs=pl.BlockSpec((tm, f), rmap),
        scratch_shapes=[pltpu.VMEM((GATHER_SLOTS, tm * rpt, LANES), jnp.uint32),
                        pltpu.SemaphoreType.DMA((GATHER_SLOTS,)),
                        pltpu.VMEM((2, d, 2 * f), BF16),
                        pltpu.VMEM((rows1, 2 * f), F32),
                        pltpu.SemaphoreType.DMA((1,))],
    )
    act = pl.pallas_call(
        functools.partial(_expert_up_kernel, tm=tm, rows=rows1),
        grid_spec=up_spec,
        out_shape=jax.ShapeDtypeStruct((n_rows, f), BF16),
        compiler_params=_params("arbitrary"),
        name="experts_up",
    )(block_e, n_used, row_tok, *tables, hp, w1, b1)

    down_spec = pltpu.PrefetchScalarGridSpec(
        num_scalar_prefetch=7,
        grid=(n_blocks,),
        in_specs=[pl.BlockSpec((tm, f), rmap),
                  pl.BlockSpec(memory_space=pl.ANY),
                  pl.BlockSpec((None, 1, d), emap)],
        out_specs=pl.BlockSpec((tm, half), rmap),
        scratch_shapes=[pltpu.VMEM((2, f, d), BF16),
                        pltpu.VMEM((f, d), F32),
                        pltpu.SemaphoreType.DMA((1,))],
    )
    return pl.pallas_call(
        _expert_down_kernel,
        grid_spec=down_spec,
        out_shape=jax.ShapeDtypeStruct((n_rows, half), jnp.uint32),
        compiler_params=_params("arbitrary"),
        name="experts_down",
    )(block_e, n_used, *tables, act, w2, b2)


def _combine_kernel(pos_ref, x_ref, w_ref, g_ref, y_hbm, o_ref, ybuf, sem, *, tc, top_k, normalize):
    i = pl.program_id(0)
    n = pl.num_programs(0)

    def row_copy(p, r, k, slot):
        return pltpu.make_async_copy(y_hbm.at[pl.ds(p, 1), :], ybuf.at[slot, k, pl.ds(r, 1), :],
                                     sem.at[slot])

    def issue(blk, slot):
        base = blk * (tc * top_k)

        def body(r, carry):
            for k in range(top_k):
                row_copy(pos_ref[base + r * top_k + k], r, k, slot).start()
            return carry

        lax.fori_loop(0, tc, body, 0)

    def wait(slot):
        for k in range(top_k):
            pltpu.make_async_copy(y_hbm.at[pl.ds(0, tc), :], ybuf.at[slot, k], sem.at[slot]).wait()

    @pl.when(i == 0)
    def _():
        issue(0, 0)
        issue(1, 1)

    slot = i % GATHER_SLOTS
    wait(slot)
    w = w_ref[...]
    x = x_ref[...]
    half = x.shape[1] // 2
    acc_lo = x[:, :half]
    acc_hi = x[:, half:]
    for k in range(top_k):
        lo, hi = _unpack_pairs(ybuf[slot, k])
        acc_lo = acc_lo + lo * w[:, k:k + 1]
        acc_hi = acc_hi + hi * w[:, k:k + 1]
    acc = jnp.concatenate([acc_lo, acc_hi], axis=1)
    if normalize:
        ms = jnp.mean(acc * acc, axis=-1, keepdims=True)
        acc = acc * lax.rsqrt(ms + NORM_EPS) * g_ref[...]
    o_ref[...] = acc
    ahead = (i + 2) % GATHER_SLOTS
    base = (i + 2) * (tc * top_k)
    for r in range(tc):
        for k in range(top_k):
            row_copy(pos_ref[base + r * top_k + k], r, k, ahead).start(priority=k % 2)

    @pl.when(i == n - 1)
    def _():
        wait((n % GATHER_SLOTS))
        wait(((n + 1) % GATHER_SLOTS))


def _combine(x, top_w, pos, y_rows, g, normalize, tc=128):
    t, d = x.shape
    tc = _tile(t, tc)
    assert t // tc >= 2
    pos = jnp.concatenate([pos.reshape(-1), jnp.zeros((2 * tc * TOP_K,), jnp.int32)])
    grid_spec = pltpu.PrefetchScalarGridSpec(
        num_scalar_prefetch=1,
        grid=(t // tc,),
        in_specs=[pl.BlockSpec((tc, d), lambda i, p: (i, 0)),
                  pl.BlockSpec((tc, TOP_K), lambda i, p: (i, 0)),
                  pl.BlockSpec((1, d), lambda i, p: (0, 0)),
                  pl.BlockSpec(memory_space=pl.ANY)],
        out_specs=pl.BlockSpec((tc, d), lambda i, p: (i, 0)),
        scratch_shapes=[pltpu.VMEM((GATHER_SLOTS, TOP_K, tc, d // 2), jnp.uint32),
                        pltpu.SemaphoreType.DMA((GATHER_SLOTS,))],
    )
    return pl.pallas_call(
        functools.partial(_combine_kernel, tc=tc, top_k=TOP_K, normalize=normalize),
        grid_spec=grid_spec,
        out_shape=jax.ShapeDtypeStruct((t, d), F32),
        compiler_params=_params("arbitrary"),
        name="combine",
    )(pos, x, top_w, g.reshape(1, d), y_rows)


def _moe_block_rows(n_assign):
    return 256 if n_assign >= 256 * 32 else 128


def kernel(x, mem, positions, norm_mix_g, w_in, w_s, b_s, ln_v_g, ln_v_b, beta_a, beta_b, w_out,
           norm_x_g, norm_mem_g, w_xq, w_xkv, w_xo, norm_ffn_g, w_router, b_router, w1, b1, w2, b2,
           norm_final_g):
    bsz, s, d = x.shape
    t = bsz * s
    depth = w_in.shape[0]
    chunk = w_s.shape[2]
    n_exp = w_router.shape[2]
    xs = x.reshape(t, d)
    mem2 = mem.reshape(-1, d)

    for l in range(depth):
        h = _rmsnorm(xs, norm_mix_g[l], BF16)
        proj = _matmul([h], w_in[l], BF16, name="w_in")
        out_a = _mixer_a(proj, w_s[l], b_s[l], ln_v_g[l], ln_v_b[l], beta_a[l])
        out_b = _mixer_b(proj, positions, beta_b[l], chunk, B_HEADS)
        xs = _matmul([out_a, out_b], w_out[l], F32, res=xs, name="w_out")

        h = _rmsnorm(xs, norm_x_g[l], BF16)
        mem_n = _rmsnorm(mem2, norm_mem_g[l], BF16)
        q = _matmul([h], w_xq[l], BF16, name="w_xq")
        kv = _matmul([mem_n], w_xkv[l], BF16, name="w_xkv")
        o = _xattn(q, kv, bsz, X_HEADS)
        xs = _matmul([o], w_xo[l], F32, res=xs, name="w_xo")

        hf, top_idx, top_w, rank, counts = _router(xs, norm_ffn_g[l], w_router[l], b_router[l])
        n_assign = t * TOP_K
        tm = _moe_block_rows(n_assign)
        counts = counts.reshape(n_exp).astype(jnp.int32)
        padded = (counts + tm - 1) // tm * tm
        pends = jnp.cumsum(padded)
        pstarts = pends - padded
        pos = pstarts[top_idx] + rank
        n_blocks = -(-n_assign // tm) + n_exp
        n_rows = n_blocks * tm
        row_tok = jnp.zeros((n_rows + tm,), jnp.int32).at[pos.reshape(-1)].set(
            jnp.arange(n_assign, dtype=jnp.int32) // TOP_K)
        block_start = jnp.arange(n_blocks, dtype=jnp.int32) * tm
        block_e = jnp.sum((pends[None, :] <= block_start[:, None]).astype(jnp.int32), axis=1)
        block_e = jnp.minimum(block_e, n_exp - 1)
        n_used = (pends[-1:] // tm).astype(jnp.int32)
        f = w2.shape[2]
        y_rows = _experts(hf, block_e, n_used, row_tok, w1[l], b1[l].reshape(n_exp, 1, 2 * f),
                          w2[l], b2[l].reshape(n_exp, 1, d), tm)
        xs = _combine(xs, top_w, pos, y_rows, norm_final_g, normalize=(l + 1 == depth))
    return xs.reshape(bsz, s, d)
```
